```python
import jax, jax.numpy as jnp
from jax import lax
import numpy as np

D_MODEL = 1024
BATCH = 2
SEQ = 8192
DEPTH = 2
DEC_BATCH = 8
DEC_SEQ = 2048
PAST_LEN = 128

HEAD_DIM = 64
N_ATTN_HEADS = 8
N_KV_HEADS = 2
ATTN_WIDTH = N_ATTN_HEADS * HEAD_DIM
KV_WIDTH = N_KV_HEADS * HEAD_DIM
N_GMLP_HEADS = 8
GMLP_WIDTH = N_GMLP_HEADS * HEAD_DIM
MIX_WIDTH = ATTN_WIDTH + GMLP_WIDTH
IN_PROJ_WIDTH = ATTN_WIDTH + 2 * KV_WIDTH + 2 * GMLP_WIDTH
WINDOW = 128
BLOCK = 128
CHUNK = 128
D_FF = 2816
EPS = 1e-6

kernel_name = "hymba_swa_gmlp_macaron_encoder"


def rms_norm(x, gain):
    x32 = x.astype(jnp.float32)
    y = x32 * lax.rsqrt(jnp.mean(x32 * x32, axis=-1, keepdims=True) + EPS)
    return (y * gain.astype(jnp.float32)).astype(x.dtype)


def swiglu(x, w_gate, w_up, w_down):
    return (jax.nn.silu(x @ w_gate) * (x @ w_up)) @ w_down


def windowed_gqa(q, k, v, sink, slopes):
    b, s = q.shape[0], q.shape[1]
    nb = s // BLOCK
    g = N_ATTN_HEADS // N_KV_HEADS
    qb = q.reshape(b, nb, BLOCK, N_KV_HEADS, g, HEAD_DIM)

    def band(t):
        tp = jnp.pad(t, ((0, 0), (BLOCK, BLOCK), (0, 0), (0, 0)))
        tp = tp.reshape(b, nb + 2, BLOCK, N_KV_HEADS, HEAD_DIM)
        return jnp.concatenate([tp[:, :-2], tp[:, 1:-1], tp[:, 2:]], axis=2)

    kb, vb = band(k), band(v)
    scores = jnp.einsum('bnqkgd,bnskd->bnkgqs', qb, kb).astype(jnp.float32) * (HEAD_DIM ** -0.5)
    qi = jnp.arange(BLOCK)[:, None]
    kj = jnp.arange(3 * BLOCK)[None, :]
    rel = kj - BLOCK - qi
    kpos = (jnp.arange(nb) * BLOCK)[:, None, None] - BLOCK + kj[None]
    valid = (jnp.abs(rel) <= WINDOW)[None] & (kpos >= 0) & (kpos < s)
    dist = jnp.abs(rel).astype(jnp.float32)
    alibi = -slopes.astype(jnp.float32).reshape(N_KV_HEADS, g)[:, :, None, None] * dist
    scores = jnp.where(valid[None, :, None, None], scores + alibi[None, None], -jnp.inf)
    sink_l = sink.astype(jnp.float32).reshape(N_KV_HEADS, g)[None, None, :, :, None, None]
    m = jnp.maximum(jnp.max(scores, axis=-1, keepdims=True), sink_l)
    p = jnp.exp(scores - m)
    denom = jnp.sum(p, axis=-1, keepdims=True) + jnp.exp(sink_l - m)
    probs = (p / denom).astype(v.dtype)
    out = jnp.einsum('bnkgqs,bnskd->bnqkgd', probs, vb)
    return out.reshape(b, s, ATTN_WIDTH)


def chunked_spatial_gating(u, gv, v_gain, w_s, b_s):
    b, s = u.shape[0], u.shape[1]
    nc = s // CHUNK
    u = jax.nn.gelu(u)
    gv = jax.nn.gelu(gv)
    gh = gv.reshape(b, s, N_GMLP_HEADS, HEAD_DIM)
    gh = rms_norm(gh, v_gain.reshape(N_GMLP_HEADS, HEAD_DIM))
    gh = gh.reshape(b, nc, CHUNK, N_GMLP_HEADS, HEAD_DIM)
    mixed = jnp.einsum('hts,bnshd->bnthd', w_s.astype(gh.dtype), gh)
    mixed = mixed + b_s.T.astype(gh.dtype)[None, None, :, :, None]
    out = u.reshape(b, nc, CHUNK, N_GMLP_HEADS, HEAD_DIM) * mixed
    return out.reshape(b, s, GMLP_WIDTH)


def trunk(x, norm_ffn1, w1_gate, w1_up, w1_down, norm_mix, w_in, sink, gmlp_v_gain,
          w_spatial, b_spatial, w_out, norm_ffn2, w2_gate, w2_up, w2_down, norm_final):
    b, s, _ = x.shape
    slopes = 2.0 ** (-8.0 * jnp.arange(1, N_ATTN_HEADS + 1, dtype=jnp.float32) / N_ATTN_HEADS)
    o1 = ATTN_WIDTH
    o2 = o1 + KV_WIDTH
    o3 = o2 + KV_WIDTH
    o4 = o3 + GMLP_WIDTH
    for l in range(DEPTH):
        h = rms_norm(x, norm_ffn1[l])
        x = x + 0.5 * swiglu(h, w1_gate[l], w1_up[l], w1_down[l])
        h = rms_norm(x, norm_mix[l])
        z = h @ w_in[l]
        q = z[..., :o1].reshape(b, s, N_ATTN_HEADS, HEAD_DIM)
        k = z[..., o1:o2].reshape(b, s, N_KV_HEADS, HEAD_DIM)
        v = z[..., o2:o3].reshape(b, s, N_KV_HEADS, HEAD_DIM)
        u = z[..., o3:o4]
        gv = z[..., o4:]
        a = windowed_gqa(q, k, v, sink[l], slopes)
        c = chunked_spatial_gating(u, gv, gmlp_v_gain[l], w_spatial[l], b_spatial[l])
        x = x + jnp.concatenate([a, c], axis=-1) @ w_out[l]
        h = rms_norm(x, norm_ffn2[l])
        x = x + 0.5 * swiglu(h, w2_gate[l], w2_up[l], w2_down[l])
    return rms_norm(x, norm_final)


def setup_inputs(seed: int = 0) -> dict:
    key = jax.random.key(seed)
    ks = jax.random.split(key, 20)
    f32 = jnp.float32

    def nrm(k, shape, scale):
        return jax.random.normal(k, shape, f32) * scale

    def gain(k, shape):
        return jnp.ones(shape, f32) + 0.02 * jax.random.normal(k, shape, f32)

    return {
        "x_prompt": jax.random.normal(ks[0], (BATCH, SEQ, D_MODEL), f32),
        "x_sample": jax.random.normal(ks[1], (DEC_BATCH, DEC_SEQ, D_MODEL), f32),
        "norm_ffn1": gain(ks[2], (DEPTH, D_MODEL)),
        "w1_gate": nrm(ks[3], (DEPTH, D_MODEL, D_FF), D_MODEL ** -0.5),
        "w1_up": nrm(ks[4], (DEPTH, D_MODEL, D_FF), D_MODEL ** -0.5),
        "w1_down": nrm(ks[5], (DEPTH, D_FF, D_MODEL), D_FF ** -0.5),
        "norm_mix": gain(ks[6], (DEPTH, D_MODEL)),
        "w_in": nrm(ks[7], (DEPTH, D_MODEL, IN_PROJ_WIDTH), D_MODEL ** -0.5),
        "sink": nrm(ks[8], (DEPTH, N_ATTN_HEADS), 0.5),
        "gmlp_v_gain": gain(ks[9], (DEPTH, GMLP_WIDTH)),
        "w_spatial": nrm(ks[10], (DEPTH, N_GMLP_HEADS, CHUNK, CHUNK), CHUNK ** -0.5),
        "b_spatial": gain(ks[11], (DEPTH, N_GMLP_HEADS, CHUNK)),
        "w_out": nrm(ks[12], (DEPTH, MIX_WIDTH, D_MODEL), MIX_WIDTH ** -0.5),
        "norm_ffn2": gain(ks[13], (DEPTH, D_MODEL)),
        "w2_gate": nrm(ks[14], (DEPTH, D_MODEL, D_FF), D_MODEL ** -0.5),
        "w2_up": nrm(ks[15], (DEPTH, D_MODEL, D_FF), D_MODEL ** -0.5),
        "w2_down": nrm(ks[16], (DEPTH, D_FF, D_MODEL), D_FF ** -0.5),
        "norm_final": gain(ks[17], (D_MODEL,)),
    }


def reference(x_prompt, x_sample, norm_ffn1, w1_gate, w1_up, w1_down, norm_mix, w_in, sink,
              gmlp_v_gain, w_spatial, b_spatial, w_out, norm_ffn2, w2_gate, w2_up, w2_down,
              norm_final):
    y_prompt = trunk(x_prompt, norm_ffn1, w1_gate, w1_up, w1_down, norm_mix, w_in, sink,
                     gmlp_v_gain, w_spatial, b_spatial, w_out, norm_ffn2, w2_gate, w2_up,
                     w2_down, norm_final)
    y_sample = trunk(x_sample, norm_ffn1, w1_gate, w1_up, w1_down, norm_mix, w_in, sink,
                     gmlp_v_gain, w_spatial, b_spatial, w_out, norm_ffn2, w2_gate, w2_up,
                     w2_down, norm_final)
    return (y_prompt, y_sample)
```

```python
import functools

import jax
import jax.numpy as jnp
from jax import lax
from jax.experimental import pallas as pl
from jax.experimental.pallas import tpu as pltpu

D_MODEL = 1024
DEPTH = 2
HEAD_DIM = 64
N_ATTN_HEADS = 8
N_KV_HEADS = 2
GROUP = N_ATTN_HEADS // N_KV_HEADS
ATTN_WIDTH = N_ATTN_HEADS * HEAD_DIM
KV_WIDTH = N_KV_HEADS * HEAD_DIM
N_GMLP_HEADS = 8
GMLP_WIDTH = N_GMLP_HEADS * HEAD_DIM
IN_PROJ_WIDTH = ATTN_WIDTH + 2 * KV_WIDTH + 2 * GMLP_WIDTH
WINDOW = 128
BLOCK = 128
CHUNK = 128
D_FF = 2816
EPS = 1e-6

LANES = 128
KEXP_WIDTH = 2 * N_KV_HEADS * LANES
BAND = 3 * BLOCK
TOKEN_TILE = 512
FF_CHUNK = D_FF // 2
VMEM_LIMIT_BYTES = 58 * 1024 * 1024

_F32 = jnp.float32
_BF16 = jnp.bfloat16


def _rms(x, gain):
    ms = jnp.mean(x * x, axis=-1, keepdims=True)
    return x * lax.rsqrt(ms + EPS) * gain


def _ffn(x, gain_ref, wgu_ref, wd_ref, act_ref):
    h = _rms(x, gain_ref[...]).astype(_BF16)
    for c in range(D_FF // FF_CHUNK):
        gu = jnp.dot(h, wgu_ref[:, 2 * c * FF_CHUNK:2 * (c + 1) * FF_CHUNK],
                     preferred_element_type=_F32)
        g = gu[:, :FF_CHUNK]
        u = gu[:, FF_CHUNK:]
        act_ref[:, c * FF_CHUNK:(c + 1) * FF_CHUNK] = (jax.nn.silu(g) * u).astype(_BF16)
    y = jnp.dot(act_ref[...], wd_ref[...], preferred_element_type=_F32)
    return x + 0.5 * y


def _mix_in_kernel(x_ref, g1_ref, wgu_ref, wd_ref, gm_ref, win_ref, vg_ref, msum_ref, wsp_ref,
                   bsp_ref, x1_ref, q_ref, kx_ref, v_ref, c_ref, act_ref):
    tm = x_ref.shape[0]
    x1 = _ffn(x_ref[...], g1_ref, wgu_ref, wd_ref, act_ref)
    x1_ref[...] = x1
    h2 = _rms(x1, gm_ref[...]).astype(_BF16)
    z = jnp.dot(h2, win_ref[...], preferred_element_type=_F32)

    o_k = ATTN_WIDTH
    o_v = o_k + KV_WIDTH
    o_u = o_v + KV_WIDTH
    o_g = o_u + GMLP_WIDTH
    q_ref[...] = (z[:, :o_k] * (HEAD_DIM ** -0.5)).astype(_BF16)
    zk = z[:, o_k:o_v]
    zr = pltpu.roll(zk, HEAD_DIM, 1)
    low = lax.broadcasted_iota(jnp.int32, (tm, LANES), 1) < HEAD_DIM
    zero = jnp.zeros_like(zk)
    kx_ref[:, 0 * LANES:1 * LANES] = jnp.where(low, zk, zero).astype(_BF16)
    kx_ref[:, 1 * LANES:2 * LANES] = jnp.where(low, zero, zr).astype(_BF16)
    kx_ref[:, 2 * LANES:3 * LANES] = jnp.where(low, zr, zero).astype(_BF16)
    kx_ref[:, 3 * LANES:4 * LANES] = jnp.where(low, zero, zk).astype(_BF16)
    v_ref[...] = z[:, o_v:o_u].astype(_BF16)

    u = jax.nn.gelu(z[:, o_u:o_g])
    gg = jax.nn.gelu(z[:, o_g:])
    sq = gg * gg
    sq_hi = sq.astype(_BF16)
    sq_lo = (sq - sq_hi.astype(_F32)).astype(_BF16)
    ssum = (jnp.dot(sq_hi, msum_ref[...], preferred_element_type=_F32)
            + jnp.dot(sq_lo, msum_ref[...], preferred_element_type=_F32))
    gh = (gg * lax.rsqrt(ssum * (1.0 / HEAD_DIM) + EPS) * vg_ref[...]).astype(_BF16)

    low_c = lax.broadcasted_iota(jnp.int32, (CHUNK, LANES), 1) < HEAD_DIM
    for ck in range(tm // CHUNK):
        r0 = ck * CHUNK
        for p in range(N_GMLP_HEADS // 2):
            c0 = p * LANES
            mm = jnp.dot(wsp_ref[p], gh[r0:r0 + CHUNK, c0:c0 + LANES], preferred_element_type=_F32)
            mixed = jnp.where(low_c, mm[:CHUNK], mm[CHUNK:])
            cc = u[r0:r0 + CHUNK, c0:c0 + LANES] * (mixed + bsp_ref[:, c0:c0 + LANES])
            c_ref[r0:r0 + CHUNK, c0:c0 + LANES] = cc.astype(_BF16)


def _mix_out_kernel(x1_ref, q_ref, kxm_ref, kxp_ref, kxn_ref, vm_ref, vp_ref, vn_ref, c_ref,
                    sink_ref, tab_ref, wo_ref, g2_ref, wgu_ref, wd_ref, gf_ref, out_ref,
                    kband_ref, vband_ref, ac_ref, act_ref, *, final_norm):
    tq = x1_ref.shape[0]
    nblk = tq // BLOCK
    t = pl.program_id(1)
    last_t = pl.num_programs(1) - 1

    kband_ref[0:BLOCK, :] = kxp_ref[...]
    kband_ref[BLOCK:BLOCK + tq, :] = kxm_ref[...]
    kband_ref[BLOCK + tq:, :] = kxn_ref[...]
    vband_ref[0:BLOCK, :] = vp_ref[...]
    vband_ref[BLOCK:BLOCK + tq, :] = vm_ref[...]
    vband_ref[BLOCK + tq:, :] = vn_ref[...]

    low = lax.broadcasted_iota(jnp.int32, (BLOCK, LANES), 1) < HEAD_DIM
    for blk in range(nblk):
        r0 = blk * BLOCK
        var = jnp.int32(0)
        if blk == nblk - 1:
            var = jnp.where(t == last_t, 2, var)
        if blk == 0:
            var = jnp.where(t == 0, 1, var)
        qb = q_ref[r0:r0 + BLOCK, :]
        for j in range(N_KV_HEADS):
            q0 = j * GROUP * HEAD_DIM
            lhs = jnp.concatenate([qb[:, q0:q0 + LANES], qb[:, q0 + LANES:q0 + 2 * LANES]], axis=0)
            probs = []
            inv_l = []
            for hl in range(2):
                kb = kband_ref[r0:r0 + BAND, (2 * j + hl) * LANES:(2 * j + hl + 1) * LANES]
                s = lax.dot_general(lhs, kb, (((1,), (1,)), ((), ())), preferred_element_type=_F32)
                for e in range(2):
                    hd = GROUP * j + 2 * e + hl
                    se = s[e * BLOCK:(e + 1) * BLOCK] + tab_ref[var, hd]
                    sk = sink_ref[hd]
                    m = jnp.maximum(jnp.max(se, axis=-1, keepdims=True), sk)
                    p = jnp.exp(se - m)
                    l = jnp.sum(p, axis=-1, keepdims=True) + jnp.exp(sk - m)
                    probs.append(p.astype(_BF16))
                    inv_l.append(1.0 / l)
            o = jnp.dot(jnp.concatenate(probs, axis=0), vband_ref[r0:r0 + BAND, :],
                        preferred_element_type=_F32)
            o = o * jnp.concatenate(inv_l, axis=0)
            for e in range(2):
                o_even = o[e * BLOCK:(e + 1) * BLOCK]
                o_odd = o[(2 + e) * BLOCK:(3 + e) * BLOCK]
                if j == 0:
                    pair = jnp.where(low, o_even, pltpu.roll(o_odd, HEAD_DIM, 1))
                else:
                    pair = jnp.where(low, pltpu.roll(o_even, HEAD_DIM, 1), o_odd)
                c0 = (2 * j + e) * LANES
                ac_ref[r0:r0 + BLOCK, c0:c0 + LANES] = pair.astype(_BF16)
    ac_ref[:, ATTN_WIDTH:] = c_ref[...]

    x2 = x1_ref[...] + jnp.dot(ac_ref[...], wo_ref[...], preferred_element_type=_F32)
    x3 = _ffn(x2, g2_ref, wgu_ref, wd_ref, act_ref)
    if final_norm:
        x3 = _rms(x3, gf_ref[...])
    out_ref[...] = x3


def _const_spec(shape):
    nd = len(shape)
    return pl.BlockSpec(shape, lambda *_: (0,) * nd, pipeline_mode=pl.Buffered(1))


def _mix_in(xf, w):
    n = xf.shape[0]
    tm = TOKEN_TILE
    row = lambda width: pl.BlockSpec((tm, width), lambda i: (i, 0))
    return pl.pallas_call(
        _mix_in_kernel,
        grid=(n // tm,),
        in_specs=[
            row(D_MODEL),
            _const_spec((1, D_MODEL)),
            _const_spec((D_MODEL, 2 * D_FF)),
            _const_spec((D_FF, D_MODEL)),
            _const_spec((1, D_MODEL)),
            _const_spec((D_MODEL, IN_PROJ_WIDTH)),
            _const_spec((1, GMLP_WIDTH)),
            _const_spec((GMLP_WIDTH, GMLP_WIDTH)),
            _const_spec((N_GMLP_HEADS // 2, 2 * CHUNK, CHUNK)),
            _const_spec((CHUNK, GMLP_WIDTH)),
        ],
        out_specs=[row(D_MODEL), row(ATTN_WIDTH), row(KEXP_WIDTH), row(KV_WIDTH), row(GMLP_WIDTH)],
        out_shape=[
            jax.ShapeDtypeStruct((n, D_MODEL), _F32),
            jax.ShapeDtypeStruct((n, ATTN_WIDTH), _BF16),
            jax.ShapeDtypeStruct((n, KEXP_WIDTH), _BF16),
            jax.ShapeDtypeStruct((n, KV_WIDTH), _BF16),
            jax.ShapeDtypeStruct((n, GMLP_WIDTH), _BF16),
        ],
        scratch_shapes=[pltpu.VMEM((tm, D_FF), _BF16)],
        compiler_params=pltpu.CompilerParams(
            dimension_semantics=("parallel",), vmem_limit_bytes=VMEM_LIMIT_BYTES),
        name="mix_in",
    )(xf, w["g1"], w["wgu1"], w["wd1"], w["gm"], w["win"], w["vg"], w["msum"], w["wsp"], w["bsp"])


def _mix_out(x1, q, kx, v, c, w, tab, gf, batch, seq, final_norm):
    n = x1.shape[0]
    tq = TOKEN_TILE
    nt = seq // tq
    r = tq // BLOCK
    nb = seq // BLOCK
    row = lambda width: pl.BlockSpec((tq, width), lambda b, t: (b * nt + t, 0))
    prev = lambda width: pl.BlockSpec(
        (BLOCK, width), lambda b, t: (b * nb + jnp.maximum(t * r - 1, 0), 0))
    nxt = lambda width: pl.BlockSpec(
        (BLOCK, width), lambda b, t: (b * nb + jnp.minimum((t + 1) * r, nb - 1), 0))
    return pl.pallas_call(
        functools.partial(_mix_out_kernel, final_norm=final_norm),
        grid=(batch, nt),
        in_specs=[
            row(D_MODEL), row(ATTN_WIDTH),
            row(KEXP_WIDTH), prev(KEXP_WIDTH), nxt(KEXP_WIDTH),
            row(KV_WIDTH), prev(KV_WIDTH), nxt(KV_WIDTH),
            row(GMLP_WIDTH),
            pl.BlockSpec(memory_space=pltpu.SMEM),
            _const_spec((3, N_ATTN_HEADS, BLOCK, BAND)),
            _const_spec((2 * ATTN_WIDTH, D_MODEL)),
            _const_spec((1, D_MODEL)),
            _const_spec((D_MODEL, 2 * D_FF)),
            _const_spec((D_FF, D_MODEL)),
            _const_spec((1, D_MODEL)),
        ],
        out_specs=row(D_MODEL),
        out_shape=jax.ShapeDtypeStruct((n, D_MODEL), _F32),
        scratch_shapes=[
            pltpu.VMEM((tq + 2 * BLOCK, KEXP_WIDTH), _BF16),
            pltpu.VMEM((tq + 2 * BLOCK, KV_WIDTH), _BF16),
            pltpu.VMEM((tq, 2 * ATTN_WIDTH), _BF16),
            pltpu.VMEM((tq, D_FF), _BF16),
        ],
        compiler_params=pltpu.CompilerParams(
            dimension_semantics=("parallel", "parallel"), vmem_limit_bytes=VMEM_LIMIT_BYTES),
        name="mix_out",
    )(x1, q, kx, kx, kx, v, v, v, c, w["sink"], tab, w["wo"], w["g2"], w["wgu2"], w["wd2"], gf)


def _interleave_gate_up(w_gate, w_up):
    parts = []
    for c in range(D_FF // FF_CHUNK):
        sl = slice(c * FF_CHUNK, (c + 1) * FF_CHUNK)
        parts += [w_gate[:, sl], w_up[:, sl]]
    return jnp.concatenate(parts, axis=1).astype(_BF16)


def _bias_table():
    slopes = 2.0 ** (-8.0 * jnp.arange(1, N_ATTN_HEADS + 1, dtype=_F32) / N_ATTN_HEADS)
    qi = jnp.arange(BLOCK)[:, None]
    kj = jnp.arange(BAND)[None, :]
    rel = kj - BLOCK - qi
    dist = jnp.abs(rel).astype(_F32)
    alibi = -slopes[:, None, None] * dist[None]
    in_window = jnp.abs(rel) <= WINDOW
    valid = jnp.stack([in_window, in_window & (kj >= BLOCK), in_window & (kj < 2 * BLOCK)])
    return jnp.where(valid[:, None], alibi[None], -jnp.inf)


def _layer_weights(l, norm_ffn1, w1_gate, w1_up, w1_down, norm_mix, w_in, sink, gmlp_v_gain,
                   w_spatial, b_spatial, w_out, norm_ffn2, w2_gate, w2_up, w2_down):
    head_of_col = jnp.arange(GMLP_WIDTH) // HEAD_DIM
    ws = w_spatial[l].astype(_BF16)
    return {
        "g1": norm_ffn1[l][None, :],
        "wgu1": _interleave_gate_up(w1_gate[l], w1_up[l]),
        "wd1": w1_down[l].astype(_BF16),
        "gm": norm_mix[l][None, :],
        "win": w_in[l].astype(_BF16),
        "vg": gmlp_v_gain[l][None, :],
        "msum": (head_of_col[:, None] == head_of_col[None, :]).astype(_BF16),
        "wsp": ws.reshape(N_GMLP_HEADS // 2, 2 * CHUNK, CHUNK),
        "bsp": jnp.repeat(b_spatial[l].T, HEAD_DIM, axis=1),
        "sink": sink[l],
        "wo": w_out[l].astype(_BF16),
        "g2": norm_ffn2[l][None, :],
        "wgu2": _interleave_gate_up(w2_gate[l], w2_up[l]),
        "wd2": w2_down[l].astype(_BF16),
    }


def _trunk(x, layers, tab, gf):
    batch, seq, _ = x.shape
    xf = x.reshape(batch * seq, D_MODEL)
    for l, w in enumerate(layers):
        x1, q, kx, v, c = _mix_in(xf, w)
        xf = _mix_out(x1, q, kx, v, c, w, tab, gf, batch, seq, final_norm=(l == DEPTH - 1))
    return xf.reshape(batch, seq, D_MODEL)


def kernel(x_prompt, x_sample, norm_ffn1, w1_gate, w1_up, w1_down, norm_mix, w_in, sink, gmlp_v_gain, w_spatial, b_spatial, w_out, norm_ffn2, w2_gate, w2_up, w2_down, norm_final):
    layers = [
        _layer_weights(l, norm_ffn1, w1_gate, w1_up, w1_down, norm_mix, w_in, sink, gmlp_v_gain,
                       w_spatial, b_spatial, w_out, norm_ffn2, w2_gate, w2_up, w2_down)
        for l in range(DEPTH)
    ]
    tab = _bias_table()
    gf = norm_final[None, :]
    return (_trunk(x_prompt, layers, tab, gf), _trunk(x_sample, layers, tab, gf))
```

```python
import functools

import jax
import jax.numpy as jnp
from jax import lax
from jax.experimental import pallas as pl
from jax.experimental.pallas import tpu as pltpu

D_MODEL = 1024
DEPTH = 2
HEAD_DIM = 64
N_ATTN_HEADS = 8
N_KV_HEADS = 2
GROUP = N_ATTN_HEADS // N_KV_HEADS
ATTN_WIDTH = N_ATTN_HEADS * HEAD_DIM
KV_WIDTH = N_KV_HEADS * HEAD_DIM
N_GMLP_HEADS = 8
GMLP_WIDTH = N_GMLP_HEADS * HEAD_DIM
IN_PROJ_WIDTH = ATTN_WIDTH + 2 * KV_WIDTH + 2 * GMLP_WIDTH
WINDOW = 128
BLOCK = 128
CHUNK = 128
D_FF = 2816
EPS = 1e-6

LANES = 128
KEXP_WIDTH = 2 * N_KV_HEADS * LANES
BAND = 3 * BLOCK
TOKEN_TILE = 512
FF_CHUNK = D_FF // 2
VMEM_LIMIT_BYTES = 58 * 1024 * 1024

_F32 = jnp.float32
_BF16 = jnp.bfloat16


def _rms(x, gain):
    ms = jnp.mean(x * x, axis=-1, keepdims=True)
    return x * lax.rsqrt(ms + EPS) * gain


def _ffn(x, gain_ref, wgu_ref, wd_ref, act_ref):
    h = _rms(x, gain_ref[...]).astype(_BF16)
    for c in range(D_FF // FF_CHUNK):
        gu = jnp.dot(h, wgu_ref[:, 2 * c * FF_CHUNK:2 * (c + 1) * FF_CHUNK],
                     preferred_element_type=_F32)
        g = gu[:, :FF_CHUNK]
        u = gu[:, FF_CHUNK:]
        act_ref[:, c * FF_CHUNK:(c + 1) * FF_CHUNK] = (jax.nn.silu(g) * u).astype(_BF16)
    y = jnp.dot(act_ref[...], wd_ref[...], preferred_element_type=_F32)
    return x + 0.5 * y


def _mix_in_kernel(x_ref, g1_ref, wgu_ref, wd_ref, gm_ref, win_ref, vg_ref, msum_ref, wsp_ref,
                   bsp_ref, x1_ref, q_ref, kx_ref, vx_ref, c_ref, act_ref):
    tm = x_ref.shape[0]
    x1 = _ffn(x_ref[...], g1_ref, wgu_ref, wd_ref, act_ref)
    x1_ref[...] = x1
    h2 = _rms(x1, gm_ref[...]).astype(_BF16)
    z = jnp.dot(h2, win_ref[...], preferred_element_type=_F32)

    o_k = ATTN_WIDTH
    o_v = o_k + KV_WIDTH
    o_u = o_v + KV_WIDTH
    o_g = o_u + GMLP_WIDTH
    q_ref[...] = (z[:, :o_k] * (HEAD_DIM ** -0.5)).astype(_BF16)
    zk = z[:, o_k:o_v]
    zr = pltpu.roll(zk, HEAD_DIM, 1)
    low = lax.broadcasted_iota(jnp.int32, (tm, LANES), 1) < HEAD_DIM
    zero = jnp.zeros_like(zk)
    kx_ref[:, 0 * LANES:1 * LANES] = jnp.where(low, zk, zero).astype(_BF16)
    kx_ref[:, 1 * LANES:2 * LANES] = jnp.where(low, zero, zr).astype(_BF16)
    kx_ref[:, 2 * LANES:3 * LANES] = jnp.where(low, zr, zero).astype(_BF16)
    kx_ref[:, 3 * LANES:4 * LANES] = jnp.where(low, zero, zk).astype(_BF16)
    zv = z[:, o_v:o_u]
    zvr = pltpu.roll(zv, HEAD_DIM, 1)
    one = jnp.ones_like(zv)
    vx_ref[:, 0 * LANES:1 * LANES] = jnp.where(low, zv, one).astype(_BF16)
    vx_ref[:, 1 * LANES:2 * LANES] = jnp.where(low, one, zvr).astype(_BF16)
    vx_ref[:, 2 * LANES:3 * LANES] = jnp.where(low, zvr, one).astype(_BF16)
    vx_ref[:, 3 * LANES:4 * LANES] = jnp.where(low, one, zv).astype(_BF16)

    u = jax.nn.gelu(z[:, o_u:o_g])
    gg = jax.nn.gelu(z[:, o_g:])
    sq = gg * gg
    sq_hi = sq.astype(_BF16)
    sq_lo = (sq - sq_hi.astype(_F32)).astype(_BF16)
    ssum = (jnp.dot(sq_hi, msum_ref[...], preferred_element_type=_F32)
            + jnp.dot(sq_lo, msum_ref[...], preferred_element_type=_F32))
    gh = (gg * lax.rsqrt(ssum * (1.0 / HEAD_DIM) + EPS) * vg_ref[...]).astype(_BF16)

    low_c = lax.broadcasted_iota(jnp.int32, (CHUNK, LANES), 1) < HEAD_DIM
    n_chunks = tm // CHUNK
    for p in range(N_GMLP_HEADS // 2):
        c0 = p * LANES
        rhs = jnp.concatenate(
            [gh[ck * CHUNK:(ck + 1) * CHUNK, c0:c0 + LANES] for ck in range(n_chunks)], axis=1)
        mm = jnp.dot(wsp_ref[p], rhs, preferred_element_type=_F32)
        for ck in range(n_chunks):
            r0 = ck * CHUNK
            l0 = ck * LANES
            mixed = jnp.where(low_c, mm[:CHUNK, l0:l0 + LANES], mm[CHUNK:, l0:l0 + LANES])
            cc = u[r0:r0 + CHUNK, c0:c0 + LANES] * (mixed + bsp_ref[:, c0:c0 + LANES])
            c_ref[r0:r0 + CHUNK, c0:c0 + LANES] = cc.astype(_BF16)


def _mix_out_kernel(x1_ref, q_ref, kxm_ref, kxp_ref, kxn_ref, vxm_ref, vxp_ref, vxn_ref, c_ref,
                    sink_ref, tab_ref, wo_ref, g2_ref, wgu_ref, wd_ref, gf_ref, out_ref,
                    kband_ref, vband_ref, ac_ref, act_ref, *, final_norm):
    tq = x1_ref.shape[0]
    nblk = tq // BLOCK
    t = pl.program_id(1)
    last_t = pl.num_programs(1) - 1

    kband_ref[0:BLOCK, :] = kxp_ref[...]
    kband_ref[BLOCK:BLOCK + tq, :] = kxm_ref[...]
    kband_ref[BLOCK + tq:, :] = kxn_ref[...]
    vband_ref[0:BLOCK, :] = vxp_ref[...]
    vband_ref[BLOCK:BLOCK + tq, :] = vxm_ref[...]
    vband_ref[BLOCK + tq:, :] = vxn_ref[...]

    low = lax.broadcasted_iota(jnp.int32, (BLOCK, LANES), 1) < HEAD_DIM
    for blk in range(nblk):
        r0 = blk * BLOCK
        var = jnp.int32(0)
        if blk == nblk - 1:
            var = jnp.where(t == last_t, 2, var)
        if blk == 0:
            var = jnp.where(t == 0, 1, var)
        qb = q_ref[r0:r0 + BLOCK, :]
        for j in range(N_KV_HEADS):
            q0 = j * GROUP * HEAD_DIM
            lhs = jnp.concatenate([qb[:, q0:q0 + LANES], qb[:, q0 + LANES:q0 + 2 * LANES]], axis=0)
            kcat = jnp.concatenate(
                [kband_ref[r0:r0 + BAND, (2 * j + hl) * LANES:(2 * j + hl + 1) * LANES]
                 for hl in range(2)], axis=0)
            s = lax.dot_general(lhs, kcat, (((1,), (1,)), ((), ())), preferred_element_type=_F32)
            probs = []
            sink_terms = []
            for hl in range(2):
                for e in range(2):
                    hd = GROUP * j + 2 * e + hl
                    se = s[e * BLOCK:(e + 1) * BLOCK, hl * BAND:(hl + 1) * BAND] + tab_ref[var, hd]
                    sk = sink_ref[hd]
                    m = jnp.maximum(jnp.max(se, axis=-1, keepdims=True), sk)
                    probs.append(jnp.exp(se - m).astype(_BF16))
                    sink_terms.append(jnp.exp(sk - m))
            pv_even = jnp.dot(jnp.concatenate(probs[:2], axis=0),
                              vband_ref[r0:r0 + BAND, (2 * j) * LANES:(2 * j + 1) * LANES],
                              preferred_element_type=_F32)
            pv_odd = jnp.dot(jnp.concatenate(probs[2:], axis=0),
                             vband_ref[r0:r0 + BAND, (2 * j + 1) * LANES:(2 * j + 2) * LANES],
                             preferred_element_type=_F32)
            for e in range(2):
                rows = slice(e * BLOCK, (e + 1) * BLOCK)
                pe = pv_even[rows]
                po = pv_odd[rows]
                denom = jnp.where(low, po + sink_terms[2 + e], pe + sink_terms[e])
                pair = jnp.where(low, pe, po) * pltpu.roll(1.0 / denom, HEAD_DIM, 1)
                c0 = (2 * j + e) * LANES
                ac_ref[r0:r0 + BLOCK, c0:c0 + LANES] = pair.astype(_BF16)
    ac_ref[:, ATTN_WIDTH:] = c_ref[...]

    x2 = x1_ref[...] + jnp.dot(ac_ref[...], wo_ref[...], preferred_element_type=_F32)
    x3 = _ffn(x2, g2_ref, wgu_ref, wd_ref, act_ref)
    if final_norm:
        x3 = _rms(x3, gf_ref[...])
    out_ref[...] = x3


def _const_spec(shape):
    nd = len(shape)
    return pl.BlockSpec(shape, lambda *_: (0,) * nd, pipeline_mode=pl.Buffered(1))


def _mix_in(xf, w):
    n = xf.shape[0]
    tm = TOKEN_TILE
    row = lambda width: pl.BlockSpec((tm, width), lambda i: (i, 0))
    return pl.pallas_call(
        _mix_in_kernel,
        grid=(n // tm,),
        in_specs=[
            row(D_MODEL),
            _const_spec((1, D_MODEL)),
            _const_spec((D_MODEL, 2 * D_FF)),
            _const_spec((D_FF, D_MODEL)),
            _const_spec((1, D_MODEL)),
            _const_spec((D_MODEL, IN_PROJ_WIDTH)),
            _const_spec((1, GMLP_WIDTH)),
            _const_spec((GMLP_WIDTH, GMLP_WIDTH)),
            _const_spec((N_GMLP_HEADS // 2, 2 * CHUNK, CHUNK)),
            _const_spec((CHUNK, GMLP_WIDTH)),
        ],
        out_specs=[row(D_MODEL), row(ATTN_WIDTH), row(KEXP_WIDTH), row(KEXP_WIDTH), row(GMLP_WIDTH)],
        out_shape=[
            jax.ShapeDtypeStruct((n, D_MODEL), _F32),
            jax.ShapeDtypeStruct((n, ATTN_WIDTH), _BF16),
            jax.ShapeDtypeStruct((n, KEXP_WIDTH), _BF16),
            jax.ShapeDtypeStruct((n, KEXP_WIDTH), _BF16),
            jax.ShapeDtypeStruct((n, GMLP_WIDTH), _BF16),
        ],
        scratch_shapes=[pltpu.VMEM((tm, D_FF), _BF16)],
        compiler_params=pltpu.CompilerParams(
            dimension_semantics=("parallel",), vmem_limit_bytes=VMEM_LIMIT_BYTES),
        name="mix_in",
    )(xf, w["g1"], w["wgu1"], w["wd1"], w["gm"], w["win"], w["vg"], w["msum"], w["wsp"], w["bsp"])


def _mix_out(x1, q, kx, v, c, w, tab, gf, batch, seq, final_norm):
    n = x1.shape[0]
    tq = TOKEN_TILE
    nt = seq // tq
    r = tq // BLOCK
    nb = seq // BLOCK
    row = lambda width: pl.BlockSpec((tq, width), lambda b, t: (b * nt + t, 0))
    prev = lambda width: pl.BlockSpec(
        (BLOCK, width), lambda b, t: (b * nb + jnp.maximum(t * r - 1, 0), 0))
    nxt = lambda width: pl.BlockSpec(
        (BLOCK, width), lambda b, t: (b * nb + jnp.minimum((t + 1) * r, nb - 1), 0))
    return pl.pallas_call(
        functools.partial(_mix_out_kernel, final_norm=final_norm),
        grid=(batch, nt),
        in_specs=[
            row(D_MODEL), row(ATTN_WIDTH),
            row(KEXP_WIDTH), prev(KEXP_WIDTH), nxt(KEXP_WIDTH),
            row(KEXP_WIDTH), prev(KEXP_WIDTH), nxt(KEXP_WIDTH),
            row(GMLP_WIDTH),
            pl.BlockSpec(memory_space=pltpu.SMEM),
            _const_spec((3, N_ATTN_HEADS, BLOCK, BAND)),
            _const_spec((2 * ATTN_WIDTH, D_MODEL)),
            _const_spec((1, D_MODEL)),
            _const_spec((D_MODEL, 2 * D_FF)),
            _const_spec((D_FF, D_MODEL)),
            _const_spec((1, D_MODEL)),
        ],
        out_specs=row(D_MODEL),
        out_shape=jax.ShapeDtypeStruct((n, D_MODEL), _F32),
        scratch_shapes=[
            pltpu.VMEM((tq + 2 * BLOCK, KEXP_WIDTH), _BF16),
            pltpu.VMEM((tq + 2 * BLOCK, KEXP_WIDTH), _BF16),
            pltpu.VMEM((tq, 2 * ATTN_WIDTH), _BF16),
            pltpu.VMEM((tq, D_FF), _BF16),
        ],
        compiler_params=pltpu.CompilerParams(
            dimension_semantics=("parallel", "parallel"), vmem_limit_bytes=VMEM_LIMIT_BYTES),
        name="mix_out",
    )(x1, q, kx, kx, kx, v, v, v, c, w["sink"], tab, w["wo"], w["g2"], w["wgu2"], w["wd2"], gf)


def _interleave_gate_up(w_gate, w_up):
    parts = []
    for c in range(D_FF // FF_CHUNK):
        sl = slice(c * FF_CHUNK, (c + 1) * FF_CHUNK)
        parts += [w_gate[:, sl], w_up[:, sl]]
    return jnp.concatenate(parts, axis=1).astype(_BF16)


def _bias_table():
    slopes = 2.0 ** (-8.0 * jnp.arange(1, N_ATTN_HEADS + 1, dtype=_F32) / N_ATTN_HEADS)
    qi = jnp.arange(BLOCK)[:, None]
    kj = jnp.arange(BAND)[None, :]
    rel = kj - BLOCK - qi
    dist = jnp.abs(rel).astype(_F32)
    alibi = -slopes[:, None, None] * dist[None]
    in_window = jnp.abs(rel) <= WINDOW
    valid = jnp.stack([in_window, in_window & (kj >= BLOCK), in_window & (kj < 2 * BLOCK)])
    return jnp.where(valid[:, None], alibi[None], -jnp.inf)


def _layer_weights(l, norm_ffn1, w1_gate, w1_up, w1_down, norm_mix, w_in, sink, gmlp_v_gain,
                   w_spatial, b_spatial, w_out, norm_ffn2, w2_gate, w2_up, w2_down):
    head_of_col = jnp.arange(GMLP_WIDTH) // HEAD_DIM
    ws = w_spatial[l].astype(_BF16)
    return {
        "g1": norm_ffn1[l][None, :],
        "wgu1": _interleave_gate_up(w1_gate[l], w1_up[l]),
        "wd1": w1_down[l].astype(_BF16),
        "gm": norm_mix[l][None, :],
        "win": w_in[l].astype(_BF16),
        "vg": gmlp_v_gain[l][None, :],
        "msum": (head_of_col[:, None] == head_of_col[None, :]).astype(_BF16),
        "wsp": ws.reshape(N_GMLP_HEADS // 2, 2 * CHUNK, CHUNK),
        "bsp": jnp.repeat(b_spatial[l].T, HEAD_DIM, axis=1),
        "sink": sink[l],
        "wo": w_out[l].astype(_BF16),
        "g2": norm_ffn2[l][None, :],
        "wgu2": _interleave_gate_up(w2_gate[l], w2_up[l]),
        "wd2": w2_down[l].astype(_BF16),
    }


def _trunk(x, layers, tab, gf):
    batch, seq, _ = x.shape
    xf = x.reshape(batch * seq, D_MODEL)
    for l, w in enumerate(layers):
        x1, q, kx, v, c = _mix_in(xf, w)
        xf = _mix_out(x1, q, kx, v, c, w, tab, gf, batch, seq, final_norm=(l == DEPTH - 1))
    return xf.reshape(batch, seq, D_MODEL)


def kernel(x_prompt, x_sample, norm_ffn1, w1_gate, w1_up, w1_down, norm_mix, w_in, sink, gmlp_v_gain, w_spatial, b_spatial, w_out, norm_ffn2, w2_gate, w2_up, w2_down, norm_final):
    layers = [
        _layer_weights(l, norm_ffn1, w1_gate, w1_up, w1_down, norm_mix, w_in, sink, gmlp_v_gain,
                       w_spatial, b_spatial, w_out, norm_ffn2, w2_gate, w2_up, w2_down)
        for l in range(DEPTH)
    ]
    tab = _bias_table()
    gf = norm_final[None, :]
    return (_trunk(x_prompt, layers, tab, gf), _trunk(x_sample, layers, tab, gf))
```

```python
import functools

import jax
import jax.numpy as jnp
from jax import lax
from jax.experimental import pallas as pl
from jax.experimental.pallas import tpu as pltpu

D_MODEL = 1024
DEPTH = 2
HEAD_DIM = 64
N_ATTN_HEADS = 8
N_KV_HEADS = 2
GROUP = N_ATTN_HEADS // N_KV_HEADS
ATTN_WIDTH = N_ATTN_HEADS * HEAD_DIM
KV_WIDTH = N_KV_HEADS * HEAD_DIM
N_GMLP_HEADS = 8
GMLP_WIDTH = N_GMLP_HEADS * HEAD_DIM
IN_PROJ_WIDTH = ATTN_WIDTH + 2 * KV_WIDTH + 2 * GMLP_WIDTH
WINDOW = 128
BLOCK = 128
CHUNK = 128
D_FF = 2816
EPS = 1e-6

LANES = 128
BF16_ROWS = 16
KEXP_WIDTH = 2 * N_KV_HEADS * LANES
BAND = 3 * BLOCK
TOKEN_TILE = 512
FF_CHUNK = 256
MIX_IN_ORDER = "b" + "aaa" + "b" + "aaaa" + "b"
MIX_OUT_ORDER = ""
CAST_STEPS = 16
VMEM_LIMIT_BYTES = 58 * 1024 * 1024

_F32 = jnp.float32
_BF16 = jnp.bfloat16


def _rms(x, gain):
    ms = jnp.mean(x * x, axis=-1, keepdims=True)
    return x * lax.rsqrt(ms + EPS) * gain


def _ffn(x_ref, res_ref, gain_ref, wg_ref, wu_ref, wd_ref, act_ref, finish):
    x = x_ref[...]
    if res_ref is not x_ref:
        res_ref[...] = x
    h = _rms(x, gain_ref[...]).astype(_BF16)
    for c in range(D_FF // FF_CHUNK):
        cols = slice(c * FF_CHUNK, (c + 1) * FF_CHUNK)
        g = jnp.dot(h, wg_ref[:, cols], preferred_element_type=_F32)
        u = jnp.dot(h, wu_ref[:, cols], preferred_element_type=_F32)
        act_ref[:, cols] = (jax.nn.silu(g) * u).astype(_BF16)
        yield
    y = jnp.dot(act_ref[...], wd_ref[...], preferred_element_type=_F32)
    finish(res_ref[...] + 0.5 * y)


def _run(order, **stages):
    live = dict(stages)

    def advance(name):
        if name in live and next(live[name], StopIteration) is StopIteration:
            del live[name]

    for name in order:
        advance(name)
    while live:
        for name in list(live):
            advance(name)


def _two_stage(step, n_tiles, stage_a, stage_b, order):
    @pl.when(step == 0)
    def _():
        _run("", a=stage_a())

    @pl.when(jnp.logical_and(step > 0, step < n_tiles))
    def _():
        _run(order, a=stage_a(), b=stage_b())

    @pl.when(step == n_tiles)
    def _():
        _run("", b=stage_b())


def _gating_and_qkv(x1s_ref, gm_ref, win_ref, vg_ref, msum_ref, wsp_ref, bsp_ref,
                    q_ref, kx_ref, vx_ref, c_ref):
    tm = x1s_ref.shape[0]
    h2 = _rms(x1s_ref[...], gm_ref[...]).astype(_BF16)
    z = jnp.dot(h2, win_ref[...], preferred_element_type=_F32)
    yield

    o_k = ATTN_WIDTH
    o_v = o_k + KV_WIDTH
    o_u = o_v + KV_WIDTH
    o_g = o_u + GMLP_WIDTH
    q_ref[...] = (z[:, :o_k] * (HEAD_DIM ** -0.5)).astype(_BF16)
    zk = z[:, o_k:o_v]
    zr = pltpu.roll(zk, HEAD_DIM, 1)
    low = lax.broadcasted_iota(jnp.int32, (tm, LANES), 1) < HEAD_DIM
    zero = jnp.zeros_like(zk)
    kx_ref[:, 0 * LANES:1 * LANES] = jnp.where(low, zk, zero).astype(_BF16)
    kx_ref[:, 1 * LANES:2 * LANES] = jnp.where(low, zero, zr).astype(_BF16)
    kx_ref[:, 2 * LANES:3 * LANES] = jnp.where(low, zr, zero).astype(_BF16)
    kx_ref[:, 3 * LANES:4 * LANES] = jnp.where(low, zero, zk).astype(_BF16)
    zv = z[:, o_v:o_u]
    zvr = pltpu.roll(zv, HEAD_DIM, 1)
    one = jnp.ones_like(zv)
    vx_ref[:, 0 * LANES:1 * LANES] = jnp.where(low, zv, one).astype(_BF16)
    vx_ref[:, 1 * LANES:2 * LANES] = jnp.where(low, one, zvr).astype(_BF16)
    vx_ref[:, 2 * LANES:3 * LANES] = jnp.where(low, zvr, one).astype(_BF16)
    vx_ref[:, 3 * LANES:4 * LANES] = jnp.where(low, one, zv).astype(_BF16)

    u = jax.nn.gelu(z[:, o_u:o_g])
    gg = jax.nn.gelu(z[:, o_g:])
    sq = gg * gg
    sq_hi = sq.astype(_BF16)
    sq_lo = (sq - sq_hi.astype(_F32)).astype(_BF16)
    ssum = (jnp.dot(sq_hi, msum_ref[...], preferred_element_type=_F32)
            + jnp.dot(sq_lo, msum_ref[...], preferred_element_type=_F32))
    gh = (gg * lax.rsqrt(ssum * (1.0 / HEAD_DIM) + EPS) * vg_ref[...]).astype(_BF16)
    yield

    low_c =lax.broadcasted_iota(jnp.int32, (CHUNK, LANES), 1) < HEAD_DIM
    n_chunks = tm // CHUNK
    for p in range(N_GMLP_HEADS // 2):
        c0 = p * LANES
        rhs = jnp.concatenate(
            [gh[ck * CHUNK:(ck + 1) * CHUNK, c0:c0 + LANES] for ck in range(n_chunks)], axis=1)
        mm = jnp.dot(wsp_ref[p], rhs, preferred_element_type=_F32)
        for ck in range(n_chunks):
            r0 = ck * CHUNK
            l0 = ck * LANES
            mixed = jnp.where(low_c, mm[:CHUNK, l0:l0 + LANES], mm[CHUNK:, l0:l0 + LANES])
            cc = u[r0:r0 + CHUNK, c0:c0 + LANES] * (mixed + bsp_ref[:, c0:c0 + LANES])
            c_ref[r0:r0 + CHUNK, c0:c0 + LANES] = cc.astype(_BF16)


def _mix_in_kernel(x_ref, g1_ref, wg_ref, wu_ref, wd_ref, gm_ref, win_ref, vg_ref, msum_ref,
                   wsp_ref, bsp_ref, x1_ref, q_ref, kx_ref, vx_ref, c_ref, act_ref, x1s_ref,
                   *, n_tiles):
    def finish(x1):
        x1_ref[...] = x1
        x1s_ref[...] = x1

    def stage_a():
        return _ffn(x_ref, x_ref, g1_ref, wg_ref, wu_ref, wd_ref, act_ref, finish)

    def stage_b():
        return _gating_and_qkv(x1s_ref, gm_ref, win_ref, vg_ref, msum_ref, wsp_ref, bsp_ref,
                               q_ref, kx_ref, vx_ref, c_ref)

    _two_stage(pl.program_id(0), n_tiles, stage_a, stage_b, order=MIX_IN_ORDER)


def _attention(t, last_t, q_ref, kxm_ref, kxp_ref, kxn_ref, vxm_ref, vxp_ref, vxn_ref, sink_ref,
               tab_ref, kband_ref, vband_ref, ac_ref, layer):
    tq = q_ref.shape[0]
    nblk = tq // BLOCK
    kband_ref[0:BLOCK, :] = kxp_ref[...]
    kband_ref[BLOCK:BLOCK + tq, :] = kxm_ref[...]
    kband_ref[BLOCK + tq:, :] = kxn_ref[...]
    vband_ref[0:BLOCK, :] = vxp_ref[...]
    vband_ref[BLOCK:BLOCK + tq, :] = vxm_ref[...]
    vband_ref[BLOCK + tq:, :] = vxn_ref[...]

    low = lax.broadcasted_iota(jnp.int32, (BLOCK, LANES), 1) < HEAD_DIM

    def scores(blk):
        r0 = blk * BLOCK
        qb = q_ref[r0:r0 + BLOCK, :]
        out = []
        for j in range(N_KV_HEADS):
            q0 = j * GROUP * HEAD_DIM
            lhs = jnp.concatenate([qb[:, q0:q0 + LANES], qb[:, q0 + LANES:q0 + 2 * LANES]], axis=0)
            kcat = jnp.concatenate(
                [kband_ref[r0:r0 + BAND, (2 * j + hl) * LANES:(2 * j + hl + 1) * LANES]
                 for hl in range(2)], axis=0)
            out.append(lax.dot_general(lhs, kcat, (((1,), (1,)), ((), ())),
                                       preferred_element_type=_F32))
        return out

    def finish(blk, j, s):
        r0 = blk * BLOCK
        var = jnp.int32(0)
        if blk == nblk - 1:
            var = jnp.where(t == last_t, 2, var)
        if blk == 0:
            var = jnp.where(t == 0, 1, var)
        probs = []
        sink_terms = []
        for hl in range(2):
            for e in range(2):
                hd = GROUP * j + 2 * e + hl
                se = s[e * BLOCK:(e + 1) * BLOCK, hl * BAND:(hl + 1) * BAND] + tab_ref[var, hd]
                sk = sink_ref[layer, hd]
                m = jnp.maximum(jnp.max(se, axis=-1, keepdims=True), sk)
                probs.append(jnp.exp(se - m).astype(_BF16))
                sink_terms.append(jnp.exp(sk - m))
        pv_even = jnp.dot(jnp.concatenate(probs[:2], axis=0),
                          vband_ref[r0:r0 + BAND, (2 * j) * LANES:(2 * j + 1) * LANES],
                          preferred_element_type=_F32)
        pv_odd = jnp.dot(jnp.concatenate(probs[2:], axis=0),
                         vband_ref[r0:r0 + BAND, (2 * j + 1) * LANES:(2 * j + 2) * LANES],
                         preferred_element_type=_F32)
        for e in range(2):
            rows = slice(e * BLOCK, (e + 1) * BLOCK)
            pe = pv_even[rows]
            po = pv_odd[rows]
            denom = jnp.where(low, po + sink_terms[2 + e], pe + sink_terms[e])
            pair = jnp.where(low, pe, po) * pltpu.roll(1.0 / denom, HEAD_DIM, 1)
            c0 = (2 * j + e) * LANES
            ac_ref[r0:r0 + BLOCK, c0:c0 + LANES] = pair.astype(_BF16)

    ahead = 2
    sc = {blk: scores(blk) for blk in range(min(ahead, nblk))}
    yield
    for blk in range(nblk):
        for j in range(N_KV_HEADS):
            if j == 0 and blk + ahead < nblk:
                sc[blk + ahead] = scores(blk + ahead)
            finish(blk, j, sc[blk][j])
            yield


def _mix_out_kernel(x1_ref, q_ref, kxm_ref, kxp_ref, kxn_ref, vxm_ref, vxp_ref, vxn_ref, c_ref,
                    sink_ref, tab_ref, wo_ref, g2_ref, wg_ref, wu_ref, wd_ref, gf_ref, out_ref,
                    kband_ref, vband_ref, ac_ref, act_ref, x2s_ref,
                    *, layer, final_norm, n_tiles, tiles_per_seq):
    step = pl.program_id(0)

    def stage_a():
        t = lax.rem(jnp.minimum(step, n_tiles - 1), tiles_per_seq)
        yield from _attention(t, tiles_per_seq - 1, q_ref, kxm_ref, kxp_ref, kxn_ref, vxm_ref,
                              vxp_ref, vxn_ref, sink_ref, tab_ref, kband_ref, vband_ref, ac_ref,
                              layer)
        ac_ref[:, ATTN_WIDTH:] = c_ref[...]
        x2s_ref[...] = x1_ref[...] + jnp.dot(ac_ref[...], wo_ref[...], preferred_element_type=_F32)

    def finish(x3):
        if final_norm:
            x3 = _rms(x3, gf_ref[...])
        out_ref[...] = x3

    def stage_b():
        return _ffn(x2s_ref, out_ref, g2_ref, wg_ref, wu_ref, wd_ref, act_ref, finish)

    _two_stage(step, n_tiles, stage_a, stage_b, order=MIX_OUT_ORDER)


def _cast_kernel(*refs):
    n = len(refs) // 2
    for src, dst in zip(refs[:n], refs[n:]):
        dst[...] = src[...].astype(_BF16)


def _cast_weights(stacks):
    specs = []
    for w in stacks:
        _, r, c = w.shape
        assert r % (CAST_STEPS * BF16_ROWS) == 0, w.shape
        specs.append(pl.BlockSpec((None, r // CAST_STEPS, c), lambda l, i: (l, i, 0)))
    return pl.pallas_call(
        _cast_kernel,
        grid=(DEPTH, CAST_STEPS),
        in_specs=specs,
        out_specs=specs,
        out_shape=[jax.ShapeDtypeStruct(w.shape, _BF16) for w in stacks],
        compiler_params=pltpu.CompilerParams(
            dimension_semantics=("parallel", "parallel"), vmem_limit_bytes=VMEM_LIMIT_BYTES),
        name="cast_weights",
    )(*stacks)


def _const_spec(shape):
    nd = len(shape)
    return pl.BlockSpec(shape, lambda *_: (0,) * nd, pipeline_mode=pl.Buffered(1))


def _layer_spec(layer, shape):
    nd = len(shape)
    return pl.BlockSpec((None,) + shape, lambda *_: (layer,) + (0,) * nd, pipeline_mode=pl.Buffered(1))


def _mix_in(xf, layer, w):
    n = xf.shape[0]
    tm = TOKEN_TILE
    n_tiles = n // tm
    cur = lambda width: pl.BlockSpec((tm, width), lambda s: (jnp.minimum(s, n_tiles - 1), 0))
    behind = lambda width: pl.BlockSpec((tm, width), lambda s: (jnp.maximum(s - 1, 0), 0))
    return pl.pallas_call(
        functools.partial(_mix_in_kernel, n_tiles=n_tiles),
        grid=(n_tiles + 1,),
        in_specs=[
            cur(D_MODEL),
            _layer_spec(layer, (1, D_MODEL)),
            _layer_spec(layer, (D_MODEL, D_FF)),
            _layer_spec(layer, (D_MODEL, D_FF)),
            _layer_spec(layer, (D_FF, D_MODEL)),
            _layer_spec(layer, (1, D_MODEL)),
            _layer_spec(layer, (D_MODEL, IN_PROJ_WIDTH)),
            _layer_spec(layer, (1, GMLP_WIDTH)),
            _const_spec((GMLP_WIDTH, GMLP_WIDTH)),
            _layer_spec(layer, (N_GMLP_HEADS // 2, 2 * CHUNK, CHUNK)),
            _layer_spec(layer, (CHUNK, GMLP_WIDTH)),
        ],
        out_specs=[cur(D_MODEL), behind(ATTN_WIDTH), behind(KEXP_WIDTH), behind(KEXP_WIDTH),
                   behind(GMLP_WIDTH)],
        out_shape=[
            jax.ShapeDtypeStruct((n, D_MODEL), _F32),
            jax.ShapeDtypeStruct((n, ATTN_WIDTH), _BF16),
            jax.ShapeDtypeStruct((n, KEXP_WIDTH), _BF16),
            jax.ShapeDtypeStruct((n, KEXP_WIDTH), _BF16),
            jax.ShapeDtypeStruct((n, GMLP_WIDTH), _BF16),
        ],
        scratch_shapes=[pltpu.VMEM((tm, D_FF), _BF16), pltpu.VMEM((tm, D_MODEL), _F32)],
        compiler_params=pltpu.CompilerParams(
            dimension_semantics=("arbitrary",), vmem_limit_bytes=VMEM_LIMIT_BYTES),
        name="mix_in",
    )(xf, w["g1"], w["wg1"], w["wu1"], w["wd1"], w["gm"], w["win"], w["vg"], w["msum"], w["wsp"],
      w["bsp"])


def _mix_out(x1, q, kx, vx, c, layer, w, seq, final_norm):
    n = x1.shape[0]
    tq = TOKEN_TILE
    n_tiles = n // tq
    nt = seq // tq
    r = tq // BLOCK
    nb = seq // BLOCK

    def tile_of(s):
        return jnp.minimum(s, n_tiles - 1)

    def prev_block(s):
        tile = tile_of(s)
        return ((tile // nt) * nb + jnp.maximum((tile % nt) * r - 1, 0), 0)

    def next_block(s):
        tile = tile_of(s)
        return ((tile // nt) * nb + jnp.minimum((tile % nt + 1) * r, nb - 1), 0)

    cur = lambda width: pl.BlockSpec((tq, width), lambda s: (tile_of(s), 0))
    prev = lambda width: pl.BlockSpec((BLOCK, width), prev_block)
    nxt = lambda width: pl.BlockSpec((BLOCK, width), next_block)
    return pl.pallas_call(
        functools.partial(_mix_out_kernel, layer=layer, final_norm=final_norm, n_tiles=n_tiles,
                          tiles_per_seq=nt),
        grid=(n_tiles + 1,),
        in_specs=[
            cur(D_MODEL), cur(ATTN_WIDTH),
            cur(KEXP_WIDTH), prev(KEXP_WIDTH), nxt(KEXP_WIDTH),
            cur(KEXP_WIDTH), prev(KEXP_WIDTH), nxt(KEXP_WIDTH),
            cur(GMLP_WIDTH),
            pl.BlockSpec(memory_space=pltpu.SMEM),
            _const_spec((3, N_ATTN_HEADS, BLOCK, BAND)),
            _layer_spec(layer, (2 * ATTN_WIDTH, D_MODEL)),
            _layer_spec(layer, (1, D_MODEL)),
            _layer_spec(layer, (D_MODEL, D_FF)),
            _layer_spec(layer, (D_MODEL, D_FF)),
            _layer_spec(layer, (D_FF, D_MODEL)),
            _const_spec((1, D_MODEL)),
        ],
        out_specs=pl.BlockSpec((tq, D_MODEL), lambda s: (jnp.maximum(s - 1, 0), 0)),
        out_shape=jax.ShapeDtypeStruct((n, D_MODEL), _F32),
        scratch_shapes=[
            pltpu.VMEM((tq + 2 * BLOCK, KEXP_WIDTH), _BF16),
            pltpu.VMEM((tq + 2 * BLOCK, KEXP_WIDTH), _BF16),
            pltpu.VMEM((tq, 2 * ATTN_WIDTH), _BF16),
            pltpu.VMEM((tq, D_FF), _BF16),
            pltpu.VMEM((tq, D_MODEL), _F32),
        ],
        compiler_params=pltpu.CompilerParams(
            dimension_semantics=("arbitrary",), vmem_limit_bytes=VMEM_LIMIT_BYTES),
        name="mix_out",
    )(x1, q, kx, kx, kx, vx, vx, vx, c, w["sink"], w["tab"], w["wo"], w["g2"], w["wg2"], w["wu2"],
      w["wd2"], w["gf"])


def _bias_table():
    slopes = 2.0 ** (-8.0 * jnp.arange(1, N_ATTN_HEADS + 1, dtype=_F32) / N_ATTN_HEADS)
    qi = jnp.arange(BLOCK)[:, None]
    kj = jnp.arange(BAND)[None, :]
    rel = kj - BLOCK - qi
    dist = jnp.abs(rel).astype(_F32)
    alibi = -slopes[:, None, None] * dist[None]
    in_window = jnp.abs(rel) <= WINDOW
    valid = jnp.stack([in_window, in_window & (kj >= BLOCK), in_window & (kj < 2 * BLOCK)])
    return jnp.where(valid[:, None], alibi[None], -jnp.inf)


def _prepare(norm_ffn1, w1_gate, w1_up, w1_down, norm_mix, w_in, sink, gmlp_v_gain, w_spatial,
             b_spatial, w_out, norm_ffn2, w2_gate, w2_up, w2_down, norm_final):
    wg1, wu1, wd1, win, wsp, wo, wg2, wu2, wd2 = _cast_weights([
        w1_gate, w1_up, w1_down, w_in,
        w_spatial.reshape(DEPTH, N_GMLP_HEADS * CHUNK, CHUNK),
        w_out, w2_gate, w2_up, w2_down])
    head_of_col = jnp.arange(GMLP_WIDTH) // HEAD_DIM
    return {
        "g1": norm_ffn1[:, None, :], "wg1": wg1, "wu1": wu1, "wd1": wd1,
        "gm": norm_mix[:, None, :], "win": win,
        "vg": gmlp_v_gain[:, None, :],
        "msum": (head_of_col[:, None] == head_of_col[None, :]).astype(_BF16),
        "wsp": wsp.reshape(DEPTH, N_GMLP_HEADS // 2, 2 * CHUNK, CHUNK),
        "bsp": jnp.repeat(jnp.swapaxes(b_spatial, 1, 2), HEAD_DIM, axis=2),
        "sink": sink, "tab": _bias_table(), "wo": wo,
        "g2": norm_ffn2[:, None, :], "wg2": wg2, "wu2": wu2, "wd2": wd2,
        "gf": norm_final[None, :],
    }


def _trunk(x, w):
    batch, seq, _ = x.shape
    xf = x.reshape(batch * seq, D_MODEL)
    for layer in range(DEPTH):
        x1, q, kx, vx, c = _mix_in(xf, layer, w)
        xf = _mix_out(x1, q, kx, vx, c, layer, w, seq, final_norm=(layer == DEPTH - 1))
    return xf.reshape(batch, seq, D_MODEL)


def kernel(x_prompt, x_sample, norm_ffn1, w1_gate, w1_up, w1_down, norm_mix, w_in, sink, gmlp_v_gain, w_spatial, b_spatial, w_out, norm_ffn2, w2_gate, w2_up, w2_down, norm_final):
    w = _prepare(norm_ffn1, w1_gate, w1_up, w1_down, norm_mix, w_in, sink, gmlp_v_gain, w_spatial,
                 b_spatial, w_out, norm_ffn2, w2_gate, w2_up, w2_down, norm_final)
    return (_trunk(x_prompt, w), _trunk(x_sample, w))
```

```python
import functools

import jax
import jax.numpy as jnp
from jax import lax
from jax.experimental import pallas as pl
from jax.experimental.pallas import tpu as pltpu

D_MODEL = 1024
DEPTH = 2
HEAD_DIM = 64
N_ATTN_HEADS = 8
N_KV_HEADS = 2
GROUP = N_ATTN_HEADS // N_KV_HEADS
ATTN_WIDTH = N_ATTN_HEADS * HEAD_DIM
KV_WIDTH = N_KV_HEADS * HEAD_DIM
N_GMLP_HEADS = 8
GMLP_WIDTH = N_GMLP_HEADS * HEAD_DIM
IN_PROJ_WIDTH = ATTN_WIDTH + 2 * KV_WIDTH + 2 * GMLP_WIDTH
WINDOW = 128
BLOCK = 128
CHUNK = 128
D_FF = 2816
EPS = 1e-6

LANES = 128
BF16_ROWS = 16
KEXP_WIDTH = 2 * N_KV_HEADS * LANES
BAND = 3 * BLOCK
TOKEN_TILE = 512
FF_CHUNK = 256
MIX_IN_ORDER = "b" + "aaa" + "b" + "aaaa" + "b"
MIX_OUT_ORDER = ""
CAST_STEPS = 16
VMEM_LIMIT_BYTES = 58 * 1024 * 1024

_F32 = jnp.float32
_BF16 = jnp.bfloat16


def _rms(x, gain):
    ms = jnp.mean(x * x, axis=-1, keepdims=True)
    return x * lax.rsqrt(ms + EPS) * gain


def _ffn(load_x, res_ref, gain_ref, wg_ref, wu_ref, wd_ref, act_ref, finish):
    x = load_x()
    if res_ref is not None:
        res_ref[...] = x
    h = _rms(x, gain_ref[...]).astype(_BF16)
    for c in range(D_FF // FF_CHUNK):
        cols = slice(c * FF_CHUNK, (c + 1) * FF_CHUNK)
        g = jnp.dot(h, wg_ref[:, cols], preferred_element_type=_F32)
        u = jnp.dot(h, wu_ref[:, cols], preferred_element_type=_F32)
        act_ref[:, cols] = (jax.nn.silu(g) * u).astype(_BF16)
        yield
    y = jnp.dot(act_ref[...], wd_ref[...], preferred_element_type=_F32)
    finish((load_x() if res_ref is None else res_ref[...]) + 0.5 * y)


def _run(order, **stages):
    live = dict(stages)

    def advance(name):
        if name in live and next(live[name], StopIteration) is StopIteration:
            del live[name]

    for name in order:
        advance(name)
    while live:
        for name in list(live):
            advance(name)


def _two_stage(step, n_tiles, stage_a, stage_b, order):
    @pl.when(step == 0)
    def _():
        _run("", a=stage_a())

    @pl.when(jnp.logical_and(step > 0, step < n_tiles))
    def _():
        _run(order, a=stage_a(), b=stage_b())

    @pl.when(step == n_tiles)
    def _():
        _run("", b=stage_b())


def _gating_and_qkv(x1s_ref, gm_ref, win_ref, vg_ref, msum_ref, wsp_ref, bsp_ref,
                    q_ref, kx_ref, vx_ref, c_ref):
    tm = x1s_ref.shape[0]
    h2 = _rms(x1s_ref[...], gm_ref[...]).astype(_BF16)
    z = jnp.dot(h2, win_ref[...], preferred_element_type=_F32)
    yield

    o_k = ATTN_WIDTH
    o_v = o_k + KV_WIDTH
    o_u = o_v + KV_WIDTH
    o_g = o_u + GMLP_WIDTH
    q_ref[...] = (z[:, :o_k] * (HEAD_DIM ** -0.5)).astype(_BF16)
    zk = z[:, o_k:o_v]
    zr = pltpu.roll(zk, HEAD_DIM, 1)
    low = lax.broadcasted_iota(jnp.int32, (tm, LANES), 1) < HEAD_DIM
    zero = jnp.zeros_like(zk)
    kx_ref[:, 0 * LANES:1 * LANES] = jnp.where(low, zk, zero).astype(_BF16)
    kx_ref[:, 1 * LANES:2 * LANES] = jnp.where(low, zero, zr).astype(_BF16)
    kx_ref[:, 2 * LANES:3 * LANES] = jnp.where(low, zr, zero).astype(_BF16)
    kx_ref[:, 3 * LANES:4 * LANES] = jnp.where(low, zero, zk).astype(_BF16)
    zv = z[:, o_v:o_u]
    zvr = pltpu.roll(zv, HEAD_DIM, 1)
    one = jnp.ones_like(zv)
    vx_ref[:, 0 * LANES:1 * LANES] = jnp.where(low, zv, one).astype(_BF16)
    vx_ref[:, 1 * LANES:2 * LANES] = jnp.where(low, one, zvr).astype(_BF16)
    vx_ref[:, 2 * LANES:3 * LANES] = jnp.where(low, zvr, one).astype(_BF16)
    vx_ref[:, 3 * LANES:4 * LANES] = jnp.where(low, one, zv).astype(_BF16)

    u = jax.nn.gelu(z[:, o_u:o_g])
    gg = jax.nn.gelu(z[:, o_g:])
    sq = gg * gg
    sq_hi = sq.astype(_BF16)
    sq_lo = (sq - sq_hi.astype(_F32)).astype(_BF16)
    ssum = (jnp.dot(sq_hi, msum_ref[...], preferred_element_type=_F32)
            + jnp.dot(sq_lo, msum_ref[...], preferred_element_type=_F32))
    gh = (gg * lax.rsqrt(ssum * (1.0 / HEAD_DIM) + EPS) * vg_ref[...]).astype(_BF16)
    yield

    low_c =lax.broadcasted_iota(jnp.int32, (CHUNK, LANES), 1) < HEAD_DIM
    n_chunks = tm // CHUNK
    for p in range(N_GMLP_HEADS // 2):
        c0 = p * LANES
        rhs = jnp.concatenate(
            [gh[ck * CHUNK:(ck + 1) * CHUNK, c0:c0 + LANES] for ck in range(n_chunks)], axis=1)
        mm = jnp.dot(wsp_ref[p], rhs, preferred_element_type=_F32)
        for ck in range(n_chunks):
            r0 = ck * CHUNK
            l0 = ck * LANES
            mixed = jnp.where(low_c, mm[:CHUNK, l0:l0 + LANES], mm[CHUNK:, l0:l0 + LANES])
            cc = u[r0:r0 + CHUNK, c0:c0 + LANES] * (mixed + bsp_ref[:, c0:c0 + LANES])
            c_ref[r0:r0 + CHUNK, c0:c0 + LANES] = cc.astype(_BF16)


def _mix_in_kernel(*refs, geo, n_inputs):
    x_refs = refs[:n_inputs]
    (g1_ref, wg_ref, wu_ref, wd_ref, gm_ref, win_ref, vg_ref, msum_ref, wsp_ref, bsp_ref,
     x1_ref, q_ref, kx_ref, vx_ref, c_ref, act_ref, x1s_ref) = refs[n_inputs:]
    step = pl.program_id(0)

    def finish(x1):
        x1_ref[...] = x1
        x1s_ref[...] = x1

    def load_x():
        x = x_refs[-1][...]
        for g in reversed(range(n_inputs - 1)):
            x = jnp.where(step < geo.tile_end(g), x_refs[g][...], x)
        return x

    def stage_a():
        return _ffn(load_x, None, g1_ref, wg_ref, wu_ref, wd_ref, act_ref, finish)

    def stage_b():
        return _gating_and_qkv(x1s_ref, gm_ref, win_ref, vg_ref, msum_ref, wsp_ref, bsp_ref,
                               q_ref, kx_ref, vx_ref, c_ref)

    _two_stage(step, geo.n_tiles, stage_a, stage_b, order=MIX_IN_ORDER)


def _attention(q_ref, kx_refs, vx_refs, sink_ref, tab_refs, ac_ref, layer):
    tq = q_ref.shape[0]
    nblk = tq // BLOCK
    assert nblk >= 2
    low = lax.broadcasted_iota(jnp.int32, (BLOCK, LANES), 1) < HEAD_DIM

    def band(refs, blk, lane_block):
        prev_ref, main_ref, next_ref = refs
        cols = slice(lane_block * LANES, (lane_block + 1) * LANES)
        lo = (blk - 1) * BLOCK
        hi = (blk + 2) * BLOCK
        pieces = []
        if lo < 0:
            pieces.append(prev_ref[:, cols])
        pieces.append(main_ref[max(lo, 0):min(hi, tq), cols])
        if hi > tq:
            pieces.append(next_ref[:, cols])
        return jnp.concatenate(pieces, axis=0)

    def scores(blk):
        r0 = blk * BLOCK
        qb = q_ref[r0:r0 + BLOCK, :]
        out = []
        for j in range(N_KV_HEADS):
            q0 = j * GROUP * HEAD_DIM
            lhs = jnp.concatenate([qb[:, q0:q0 + LANES], qb[:, q0 + LANES:q0 + 2 * LANES]], axis=0)
            kcat = jnp.concatenate([band(kx_refs, blk, 2 * j + hl) for hl in range(2)], axis=0)
            out.append(lax.dot_general(lhs, kcat, (((1,), (1,)), ((), ())),
                                       preferred_element_type=_F32))
        return out

    def softmax(blk, j, s):
        tabp_ref, tab_ref, tabn_ref = tab_refs

        def bias(hd):
            if blk == 0:
                return jnp.concatenate([tabp_ref[hd], tab_ref[hd, :, BLOCK:]], axis=1)
            if blk == nblk - 1:
                return jnp.concatenate([tab_ref[hd, :, :2 * BLOCK], tabn_ref[hd]], axis=1)
            return tab_ref[hd]

        probs = []
        sink_terms = []
        for hl in range(2):
            for e in range(2):
                hd = GROUP * j + 2 * e + hl
                se = s[e * BLOCK:(e + 1) * BLOCK, hl * BAND:(hl + 1) * BAND] + bias(hd)
                sk = sink_ref[layer, hd]
                m = jnp.maximum(jnp.max(se, axis=-1, keepdims=True), sk)
                probs.append(jnp.exp(se - m).astype(_BF16))
                sink_terms.append(jnp.exp(sk - m))
        return probs, sink_terms

    def weighted_values(blk, j, probs, sink_terms):
        r0 = blk * BLOCK
        pv_even = jnp.dot(jnp.concatenate(probs[:2], axis=0), band(vx_refs, blk, 2 * j),
                          preferred_element_type=_F32)
        pv_odd = jnp.dot(jnp.concatenate(probs[2:], axis=0), band(vx_refs, blk, 2 * j + 1),
                         preferred_element_type=_F32)
        for e in range(2):
            rows = slice(e * BLOCK, (e + 1) * BLOCK)
            pe = pv_even[rows]
            po = pv_odd[rows]
            denom = jnp.where(low, po + sink_terms[2 + e], pe + sink_terms[e])
            pair = jnp.where(low, pe, po) * pltpu.roll(1.0 / denom, HEAD_DIM, 1)
            c0 = (2 * j + e) * LANES
            ac_ref[r0:r0 + BLOCK, c0:c0 + LANES] = pair.astype(_BF16)

    ahead = 2
    sc = {blk: scores(blk) for blk in range(min(ahead, nblk))}
    yield
    ready = None
    for blk in range(nblk):
        for j in range(N_KV_HEADS):
            if j == 0 and blk + ahead < nblk:
                sc[blk + ahead] = scores(blk + ahead)
            fresh = (blk, j) + softmax(blk, j, sc[blk][j])
            if ready is not None:
                weighted_values(*ready)
            ready = fresh
            yield
    weighted_values(*ready)


def _mix_out_kernel(x1_ref, q_ref, kxp_ref, kxm_ref, kxn_ref, vxp_ref, vxm_ref, vxn_ref, c_ref,
                    sink_ref, tabf_ref, tabi_ref, tabl_ref, wo_ref, g2_ref, wg_ref, wu_ref, wd_ref,
                    gf_ref, *refs, geo, layer, n_outputs):
    out_refs = refs[:n_outputs]
    ac_ref, act_ref, x2s_ref, res_ref = refs[n_outputs:]
    step = pl.program_id(0)

    def stage_a():
        yield from _attention(q_ref, (kxp_ref, kxm_ref, kxn_ref), (vxp_ref, vxm_ref, vxn_ref),
                              sink_ref, (tabf_ref, tabi_ref, tabl_ref), ac_ref, layer)
        ac_ref[:, ATTN_WIDTH:] = c_ref[...]
        x2s_ref[...] = x1_ref[...] + jnp.dot(ac_ref[...], wo_ref[...], preferred_element_type=_F32)

    def finish(x3):
        if n_outputs == 1:
            out_refs[0][...] = x3
            return
        x3 = _rms(x3, gf_ref[...])
        tile = step - 1
        for g in range(n_outputs):
            @pl.when(jnp.logical_and(tile >= geo.tile_start(g), tile < geo.tile_end(g)))
            def _(g=g):
                out_refs[g][...] = x3

    def stage_b():
        return _ffn(lambda: x2s_ref[...], res_ref, g2_ref, wg_ref, wu_ref, wd_ref, act_ref, finish)

    _two_stage(step, geo.n_tiles, stage_a, stage_b, order=MIX_OUT_ORDER)


def _cast_kernel(*refs):
    n = len(refs) // 2
    for src, dst in zip(refs[:n], refs[n:]):
        dst[...] = src[...].astype(_BF16)


def _cast_weights(stacks):
    specs = []
    for w in stacks:
        _, r, c = w.shape
        assert r % (CAST_STEPS * BF16_ROWS) == 0, w.shape
        specs.append(pl.BlockSpec((None, r // CAST_STEPS, c), lambda l, i: (l, i, 0)))
    return pl.pallas_call(
        _cast_kernel,
        grid=(DEPTH, CAST_STEPS),
        in_specs=specs,
        out_specs=specs,
        out_shape=[jax.ShapeDtypeStruct(w.shape, _BF16) for w in stacks],
        compiler_params=pltpu.CompilerParams(
            dimension_semantics=("parallel", "parallel"), vmem_limit_bytes=VMEM_LIMIT_BYTES),
        name="cast_weights",
    )(*stacks)


def _const_spec(shape):
    nd = len(shape)
    return pl.BlockSpec(shape, lambda *_: (0,) * nd, pipeline_mode=pl.Buffered(1))


def _layer_spec(layer, shape):
    nd = len(shape)
    return pl.BlockSpec((None,) + shape, lambda *_: (layer,) + (0,) * nd, pipeline_mode=pl.Buffered(1))


class _Geometry:
    def __init__(self, shapes):
        self.tiles = [b * s // TOKEN_TILE for b, s in shapes]
        self.per_seq = [s // TOKEN_TILE for _, s in shapes]
        assert all(s % TOKEN_TILE == 0 for _, s in shapes), shapes
        self.n_tiles = sum(self.tiles)
        self.rows = self.n_tiles * TOKEN_TILE

    def tile_start(self, g):
        return sum(self.tiles[:g])

    def tile_end(self, g):
        return sum(self.tiles[:g + 1])

    def tile_of(self, step):
        return jnp.minimum(step, self.n_tiles - 1)

    def group_tile(self, tile, g):
        return jnp.clip(tile - self.tile_start(g), 0, self.tiles[g] - 1)

    def seq_edges(self, tile):
        first = last = None
        for g in reversed(range(len(self.tiles))):
            pos = (tile - self.tile_start(g)) % self.per_seq[g]
            f, l = pos == 0, pos == self.per_seq[g] - 1
            if first is None:
                first, last = f, l
            else:
                inside = tile < self.tile_end(g)
                first, last = jnp.where(inside, f, first), jnp.where(inside, l, last)
        return first, last


def _mix_in(xs, layer, w, geo):
    tm = TOKEN_TILE
    n = geo.rows
    if len(xs) == 1:
        x_specs = [pl.BlockSpec((tm, D_MODEL), lambda s: (geo.tile_of(s), 0))]
    else:
        x_specs = [pl.BlockSpec((tm, D_MODEL), lambda s, g=g: (geo.group_tile(geo.tile_of(s), g), 0))
                   for g in range(len(xs))]
    cur = lambda width: pl.BlockSpec((tm, width), lambda s: (geo.tile_of(s), 0))
    behind = lambda width: pl.BlockSpec((tm, width), lambda s: (jnp.maximum(s - 1, 0), 0))
    return pl.pallas_call(
        functools.partial(_mix_in_kernel, geo=geo, n_inputs=len(xs)),
        grid=(geo.n_tiles + 1,),
        in_specs=x_specs + [
            _layer_spec(layer, (1, D_MODEL)),
            _layer_spec(layer, (D_MODEL, D_FF)),
            _layer_spec(layer, (D_MODEL, D_FF)),
            _layer_spec(layer, (D_FF, D_MODEL)),
            _layer_spec(layer, (1, D_MODEL)),
            _layer_spec(layer, (D_MODEL, IN_PROJ_WIDTH)),
            _layer_spec(layer, (1, GMLP_WIDTH)),
            _const_spec((GMLP_WIDTH, GMLP_WIDTH)),
            _layer_spec(layer, (N_GMLP_HEADS // 2, 2 * CHUNK, CHUNK)),
            _layer_spec(layer, (CHUNK, GMLP_WIDTH)),
        ],
        out_specs=[cur(D_MODEL), behind(ATTN_WIDTH), behind(KEXP_WIDTH), behind(KEXP_WIDTH),
                   behind(GMLP_WIDTH)],
        out_shape=[
            jax.ShapeDtypeStruct((n, D_MODEL), _F32),
            jax.ShapeDtypeStruct((n, ATTN_WIDTH), _BF16),
            jax.ShapeDtypeStruct((n, KEXP_WIDTH), _BF16),
            jax.ShapeDtypeStruct((n, KEXP_WIDTH), _BF16),
            jax.ShapeDtypeStruct((n, GMLP_WIDTH), _BF16),
        ],
        scratch_shapes=[pltpu.VMEM((tm, D_FF), _BF16), pltpu.VMEM((tm, D_MODEL), _F32)],
        compiler_params=pltpu.CompilerParams(
            dimension_semantics=("arbitrary",), vmem_limit_bytes=VMEM_LIMIT_BYTES),
        name="mix_in",
    )(*xs, w["g1"], w["wg1"], w["wu1"], w["wd1"], w["gm"], w["win"], w["vg"], w["msum"], w["wsp"],
      w["bsp"])


def _mix_out(x1, q, kx, vx, c, layer, w, geo, last_layer):
    tq = TOKEN_TILE
    r = tq // BLOCK

    def prev_block(s):
        tile = geo.tile_of(s)
        return (tile * r - jnp.where(geo.seq_edges(tile)[0], 0, 1), 0)

    def next_block(s):
        tile = geo.tile_of(s)
        return ((tile + 1) * r - jnp.where(geo.seq_edges(tile)[1], 1, 0), 0)

    edge = (None, N_ATTN_HEADS, BLOCK, BLOCK)
    tab_first = pl.BlockSpec(
        edge, lambda s: (jnp.where(geo.seq_edges(geo.tile_of(s))[0], 1, 0), 0, 0, 0))
    tab_inner = _const_spec((N_ATTN_HEADS, BLOCK, BAND))
    tab_last = pl.BlockSpec(
        edge, lambda s: (jnp.where(geo.seq_edges(geo.tile_of(s))[1], 1, 0), 0, 0, 0))

    cur = lambda width: pl.BlockSpec((tq, width), lambda s: (geo.tile_of(s), 0))
    prev = lambda width: pl.BlockSpec((BLOCK, width), prev_block)
    nxt = lambda width: pl.BlockSpec((BLOCK, width), next_block)
    if last_layer:
        out_specs = [pl.BlockSpec((tq, D_MODEL), lambda s, g=g: (geo.group_tile(s - 1, g), 0))
                     for g in range(len(geo.tiles))]
        out_shape = [jax.ShapeDtypeStruct((t * tq, D_MODEL), _F32) for t in geo.tiles]
    else:
        out_specs = [pl.BlockSpec((tq, D_MODEL), lambda s: (jnp.maximum(s - 1, 0), 0))]
        out_shape = [jax.ShapeDtypeStruct((geo.rows, D_MODEL), _F32)]
    return pl.pallas_call(
        functools.partial(_mix_out_kernel, geo=geo, layer=layer, n_outputs=len(out_specs)),
        grid=(geo.n_tiles + 1,),
        in_specs=[
            cur(D_MODEL), cur(ATTN_WIDTH),
            prev(KEXP_WIDTH), cur(KEXP_WIDTH), nxt(KEXP_WIDTH),
            prev(KEXP_WIDTH), cur(KEXP_WIDTH), nxt(KEXP_WIDTH),
            cur(GMLP_WIDTH),
            pl.BlockSpec(memory_space=pltpu.SMEM),
            tab_first, tab_inner, tab_last,
            _layer_spec(layer, (2 * ATTN_WIDTH, D_MODEL)),
            _layer_spec(layer, (1, D_MODEL)),
            _layer_spec(layer, (D_MODEL, D_FF)),
            _layer_spec(layer, (D_MODEL, D_FF)),
            _layer_spec(layer, (D_FF, D_MODEL)),
            _const_spec((1, D_MODEL)),
        ],
        out_specs=out_specs,
        out_shape=out_shape,
        scratch_shapes=[
            pltpu.VMEM((tq, 2 * ATTN_WIDTH), _BF16),
            pltpu.VMEM((tq, D_FF), _BF16),
            pltpu.VMEM((tq, D_MODEL), _F32),
            pltpu.VMEM((tq, D_MODEL), _F32),
        ],
        compiler_params=pltpu.CompilerParams(
            dimension_semantics=("arbitrary",), vmem_limit_bytes=VMEM_LIMIT_BYTES),
        name="mix_out",
    )(x1, q, kx, kx, kx, vx, vx, vx, c, w["sink"], w["tab_prev"], w["tab"], w["tab_next"], w["wo"], w["g2"],
      w["wg2"], w["wu2"], w["wd2"], w["gf"])


def _bias_tables():
    slopes = 2.0 ** (-8.0 * jnp.arange(1, N_ATTN_HEADS + 1, dtype=_F32) / N_ATTN_HEADS)
    qi = jnp.arange(BLOCK)[:, None]
    kj = jnp.arange(BAND)[None, :]
    rel = kj - BLOCK - qi
    dist = jnp.abs(rel).astype(_F32)
    alibi = -slopes[:, None, None] * dist[None]
    table = jnp.where((jnp.abs(rel) <= WINDOW)[None], alibi, -jnp.inf)
    absent = jnp.full((N_ATTN_HEADS, BLOCK, BLOCK), -jnp.inf, _F32)
    return (table, jnp.stack([table[:, :, :BLOCK], absent]),
            jnp.stack([table[:, :, 2 * BLOCK:], absent]))


def _prepare(norm_ffn1, w1_gate, w1_up, w1_down, norm_mix, w_in, sink, gmlp_v_gain, w_spatial,
             b_spatial, w_out, norm_ffn2, w2_gate, w2_up, w2_down, norm_final):
    wg1, wu1, wd1, win, wsp, wo, wg2, wu2, wd2 = _cast_weights([
        w1_gate, w1_up, w1_down, w_in,
        w_spatial.reshape(DEPTH, N_GMLP_HEADS * CHUNK, CHUNK),
        w_out, w2_gate, w2_up, w2_down])
    head_of_col = jnp.arange(GMLP_WIDTH) // HEAD_DIM
    tab, tab_prev, tab_next = _bias_tables()
    return {
        "g1": norm_ffn1[:, None, :], "wg1": wg1, "wu1": wu1, "wd1": wd1,
        "gm": norm_mix[:, None, :], "win": win,
        "vg": gmlp_v_gain[:, None, :],
        "msum": (head_of_col[:, None] == head_of_col[None, :]).astype(_BF16),
        "wsp": wsp.reshape(DEPTH, N_GMLP_HEADS // 2, 2 * CHUNK, CHUNK),
        "bsp": jnp.repeat(jnp.swapaxes(b_spatial, 1, 2), HEAD_DIM, axis=2),
        "sink": sink, "tab": tab, "tab_prev": tab_prev, "tab_next": tab_next, "wo": wo,
        "g2": norm_ffn2[:, None, :], "wg2": wg2, "wu2": wu2, "wd2": wd2,
        "gf": norm_final[None, :],
    }


def _forward(xs, w):
    geo = _Geometry([x.shape[:2] for x in xs])
    acts = [x.reshape(-1, D_MODEL) for x in xs]
    for layer in range(DEPTH):
        x1, q, kx, vx, c = _mix_in(acts, layer, w, geo)
        acts = _mix_out(x1, q, kx, vx, c, layer, w, geo, last_layer=(layer == DEPTH - 1))
    return tuple(y.reshape(x.shape) for x, y in zip(xs, acts))


def kernel(x_prompt, x_sample, norm_ffn1, w1_gate, w1_up, w1_down, norm_mix, w_in, sink, gmlp_v_gain, w_spatial, b_spatial, w_out, norm_ffn2, w2_gate, w2_up, w2_down, norm_final):
    w = _prepare(norm_ffn1, w1_gate, w1_up, w1_down, norm_mix, w_in, sink, gmlp_v_gain, w_spatial,
                 b_spatial, w_out, norm_ffn2, w2_gate, w2_up, w2_down, norm_final)
    return _forward((x_prompt, x_sample), w)
```

```python
import functools

import jax
import jax.numpy as jnp
from jax import lax
from jax.experimental import pallas as pl
from jax.experimental.pallas import tpu as pltpu

D_MODEL = 1024
DEPTH = 2
HEAD_DIM = 64
N_ATTN_HEADS = 8
N_KV_HEADS = 2
GROUP = N_ATTN_HEADS // N_KV_HEADS
ATTN_WIDTH = N_ATTN_HEADS * HEAD_DIM
KV_WIDTH = N_KV_HEADS * HEAD_DIM
N_GMLP_HEADS = 8
GMLP_WIDTH = N_GMLP_HEADS * HEAD_DIM
IN_PROJ_WIDTH = ATTN_WIDTH + 2 * KV_WIDTH + 2 * GMLP_WIDTH
WINDOW = 128
BLOCK = 128
CHUNK = 128
D_FF = 2816
EPS = 1e-6

LANES = 128
BF16_ROWS = 16
KEXP_WIDTH = 2 * N_KV_HEADS * LANES
BAND = 3 * BLOCK
TOKEN_TILE = 512
FF_CHUNK = 256
MIX_IN_ORDER = "ab" + "aaa" + "b" + "aaaa" + "b"
MIX_OUT_ORDER = "bb" + "ab" * 9
CAST_STEPS = 16
VMEM_LIMIT_BYTES = 58 * 1024 * 1024

_F32 = jnp.float32
_BF16 = jnp.bfloat16


def _rms(x, gain):
    ms = jnp.mean(x * x, axis=-1, keepdims=True)
    return x * lax.rsqrt(ms + EPS) * gain


def _ffn(load_x, res_ref, gain_ref, wg_ref, wu_ref, wd_ref, act_ref, finish):
    x = load_x()
    if res_ref is not None:
        res_ref[...] = x
    h = _rms(x, gain_ref[...]).astype(_BF16)
    for c in range(D_FF // FF_CHUNK):
        cols = slice(c * FF_CHUNK, (c + 1) * FF_CHUNK)
        g = jnp.dot(h, wg_ref[:, cols], preferred_element_type=_F32)
        u = jnp.dot(h, wu_ref[:, cols], preferred_element_type=_F32)
        act_ref[:, cols] = (jax.nn.silu(g) * u).astype(_BF16)
        yield
    y = jnp.dot(act_ref[...], wd_ref[...], preferred_element_type=_F32)
    finish((load_x() if res_ref is None else res_ref[...]) + 0.5 * y)


def _run(order, **stages):
    live = dict(stages)

    def advance(name):
        if name in live and next(live[name], StopIteration) is StopIteration:
            del live[name]

    for name in order:
        advance(name)
    while live:
        for name in list(live):
            advance(name)


def _two_stage(step, n_tiles, stage_a, stage_b, order):
    @pl.when(step == 0)
    def _():
        _run("", a=stage_a())

    @pl.when(jnp.logical_and(step > 0, step < n_tiles))
    def _():
        _run(order, a=stage_a(), b=stage_b())

    @pl.when(step == n_tiles)
    def _():
        _run("", b=stage_b())


def _gating_and_qkv(x1s_ref, gm_ref, win_ref, vg_ref, msum_ref, wsp_ref, bsp_ref,
                    q_ref, kx_ref, vx_ref, c_ref):
    tm = x1s_ref.shape[0]
    h2 = _rms(x1s_ref[...], gm_ref[...]).astype(_BF16)
    z = jnp.dot(h2, win_ref[...], preferred_element_type=_F32)
    yield

    o_k = ATTN_WIDTH
    o_v = o_k + KV_WIDTH
    o_u = o_v + KV_WIDTH
    o_g = o_u + GMLP_WIDTH
    q_ref[...] = (z[:, :o_k] * (HEAD_DIM ** -0.5)).astype(_BF16)
    zk = z[:, o_k:o_v]
    zr = pltpu.roll(zk, HEAD_DIM, 1)
    low = lax.broadcasted_iota(jnp.int32, (tm, LANES), 1) < HEAD_DIM
    zero = jnp.zeros_like(zk)
    kx_ref[:, 0 * LANES:1 * LANES] = jnp.where(low, zk, zero).astype(_BF16)
    kx_ref[:, 1 * LANES:2 * LANES] = jnp.where(low, zero, zr).astype(_BF16)
    kx_ref[:, 2 * LANES:3 * LANES] = jnp.where(low, zr, zero).astype(_BF16)
    kx_ref[:, 3 * LANES:4 * LANES] = jnp.where(low, zero, zk).astype(_BF16)
    zv = z[:, o_v:o_u]
    zvr = pltpu.roll(zv, HEAD_DIM, 1)
    one = jnp.ones_like(zv)
    vx_ref[:, 0 * LANES:1 * LANES] = jnp.where(low, zv, one).astype(_BF16)
    vx_ref[:, 1 * LANES:2 * LANES] = jnp.where(low, one, zvr).astype(_BF16)
    vx_ref[:, 2 * LANES:3 * LANES] = jnp.where(low, zvr, one).astype(_BF16)
    vx_ref[:, 3 * LANES:4 * LANES] = jnp.where(low, one, zv).astype(_BF16)

    u = jax.nn.gelu(z[:, o_u:o_g])
    gg = jax.nn.gelu(z[:, o_g:])
    sq = gg * gg
    sq_hi = sq.astype(_BF16)
    sq_lo = (sq - sq_hi.astype(_F32)).astype(_BF16)
    ssum = (jnp.dot(sq_hi, msum_ref[...], preferred_element_type=_F32)
            + jnp.dot(sq_lo, msum_ref[...], preferred_element_type=_F32))
    gh = (gg * lax.rsqrt(ssum * (1.0 / HEAD_DIM) + EPS) * vg_ref[...]).astype(_BF16)
    yield

    low_c =lax.broadcasted_iota(jnp.int32, (CHUNK, LANES), 1) < HEAD_DIM
    n_chunks = tm // CHUNK
    for p in range(N_GMLP_HEADS // 2):
        c0 = p * LANES
        rhs = jnp.concatenate(
            [gh[ck * CHUNK:(ck + 1) * CHUNK, c0:c0 + LANES] for ck in range(n_chunks)], axis=1)
        mm = jnp.dot(wsp_ref[p], rhs, preferred_element_type=_F32)
        for ck in range(n_chunks):
            r0 = ck * CHUNK
            l0 = ck * LANES
            mixed = jnp.where(low_c, mm[:CHUNK, l0:l0 + LANES], mm[CHUNK:, l0:l0 + LANES])
            cc = u[r0:r0 + CHUNK, c0:c0 + LANES] * (mixed + bsp_ref[:, c0:c0 + LANES])
            c_ref[r0:r0 + CHUNK, c0:c0 + LANES] = cc.astype(_BF16)


def _mix_in_kernel(*refs, geo, n_inputs):
    x_refs = refs[:n_inputs]
    (g1_ref, wg_ref, wu_ref, wd_ref, gm_ref, win_ref, vg_ref, msum_ref, wsp_ref, bsp_ref,
     x1_ref, q_ref, kx_ref, vx_ref, c_ref, act_ref, x1s_ref) = refs[n_inputs:]
    step = pl.program_id(0)

    def finish(x1):
        x1_ref[...] = x1
        x1s_ref[...] = x1

    def load_x():
        x = x_refs[-1][...]
        for g in reversed(range(n_inputs - 1)):
            x = jnp.where(step < geo.tile_end(g), x_refs[g][...], x)
        return x

    def stage_a():
        return _ffn(load_x, None, g1_ref, wg_ref, wu_ref, wd_ref, act_ref, finish)

    def stage_b():
        return _gating_and_qkv(x1s_ref, gm_ref, win_ref, vg_ref, msum_ref, wsp_ref, bsp_ref,
                               q_ref, kx_ref, vx_ref, c_ref)

    _two_stage(step, geo.n_tiles, stage_a, stage_b, order=MIX_IN_ORDER)


def _attention(q_ref, kx_refs, vx_refs, sink_ref, tab_refs, ac_ref, layer):
    tq = q_ref.shape[0]
    nblk = tq // BLOCK
    assert nblk >= 2
    low = lax.broadcasted_iota(jnp.int32, (BLOCK, LANES), 1) < HEAD_DIM

    def band(refs, blk, lane_block):
        prev_ref, main_ref, next_ref = refs
        cols = slice(lane_block * LANES, (lane_block + 1) * LANES)
        lo = (blk - 1) * BLOCK
        hi = (blk + 2) * BLOCK
        pieces = []
        if lo < 0:
            pieces.append(prev_ref[:, cols])
        pieces.append(main_ref[max(lo, 0):min(hi, tq), cols])
        if hi > tq:
            pieces.append(next_ref[:, cols])
        return jnp.concatenate(pieces, axis=0)

    def scores(blk):
        r0 = blk * BLOCK
        qb = q_ref[r0:r0 + BLOCK, :]
        out = []
        for j in range(N_KV_HEADS):
            q0 = j * GROUP * HEAD_DIM
            lhs = jnp.concatenate([qb[:, q0:q0 + LANES], qb[:, q0 + LANES:q0 + 2 * LANES]], axis=0)
            kcat = jnp.concatenate([band(kx_refs, blk, 2 * j + hl) for hl in range(2)], axis=0)
            out.append(lax.dot_general(lhs, kcat, (((1,), (1,)), ((), ())),
                                       preferred_element_type=_F32))
        return out

    def softmax(blk, j, s):
        tabp_ref, tab_ref, tabn_ref = tab_refs

        def bias(hd):
            if blk == 0:
                return jnp.concatenate([tabp_ref[hd], tab_ref[hd, :, BLOCK:]], axis=1)
            if blk == nblk - 1:
                return jnp.concatenate([tab_ref[hd, :, :2 * BLOCK], tabn_ref[hd]], axis=1)
            return tab_ref[hd]

        probs = []
        sink_terms = []
        for hl in range(2):
            for e in range(2):
                hd = GROUP * j + 2 * e + hl
                se = s[e * BLOCK:(e + 1) * BLOCK, hl * BAND:(hl + 1) * BAND] + bias(hd)
                sk = sink_ref[layer, hd]
                m = jnp.maximum(jnp.max(se, axis=-1, keepdims=True), sk)
                probs.append(jnp.exp(se - m).astype(_BF16))
                sink_terms.append(jnp.exp(sk - m))
        return probs, sink_terms

    def weighted_values(blk, j, probs, sink_terms):
        r0 = blk * BLOCK
        pv_even = jnp.dot(jnp.concatenate(probs[:2], axis=0), band(vx_refs, blk, 2 * j),
                          preferred_element_type=_F32)
        pv_odd = jnp.dot(jnp.concatenate(probs[2:], axis=0), band(vx_refs, blk, 2 * j + 1),
                         preferred_element_type=_F32)
        for e in range(2):
            rows = slice(e * BLOCK, (e + 1) * BLOCK)
            pe = pv_even[rows]
            po = pv_odd[rows]
            denom = jnp.where(low, po + sink_terms[2 + e], pe + sink_terms[e])
            pair = jnp.where(low, pe, po) * pltpu.roll(1.0 / denom, HEAD_DIM, 1)
            c0 = (2 * j + e) * LANES
            ac_ref[r0:r0 + BLOCK, c0:c0 + LANES] = pair.astype(_BF16)

    ahead = 2
    sc = {blk: scores(blk) for blk in range(min(ahead, nblk))}
    yield
    ready = None
    for blk in range(nblk):
        for j in range(N_KV_HEADS):
            if j == 0 and blk + ahead < nblk:
                sc[blk + ahead] = scores(blk + ahead)
            fresh = (blk, j) + softmax(blk, j, sc[blk][j])
            if ready is not None:
                weighted_values(*ready)
            ready = fresh
            yield
    weighted_values(*ready)


def _mix_out_kernel(x1_ref, q_ref, kxp_ref, kxm_ref, kxn_ref, vxp_ref, vxm_ref, vxn_ref, c_ref,
                    sink_ref, tabf_ref, tabi_ref, tabl_ref, wo_ref, g2_ref, wg_ref, wu_ref, wd_ref,
                    gf_ref, *refs, geo, layer, n_outputs):
    out_refs = refs[:n_outputs]
    ac_ref, act_ref, x2s_ref, res_ref = refs[n_outputs:]
    step = pl.program_id(0)

    def stage_a():
        yield from _attention(q_ref, (kxp_ref, kxm_ref, kxn_ref), (vxp_ref, vxm_ref, vxn_ref),
                              sink_ref, (tabf_ref, tabi_ref, tabl_ref), ac_ref, layer)
        ac_ref[:, ATTN_WIDTH:] = c_ref[...]
        x2s_ref[...] = x1_ref[...] + jnp.dot(ac_ref[...], wo_ref[...], preferred_element_type=_F32)

    def finish(x3):
        if n_outputs == 1:
            out_refs[0][...] = x3
            return
        x3 = _rms(x3, gf_ref[...])
        tile = step - 1
        for g in range(n_outputs):
            @pl.when(jnp.logical_and(tile >= geo.tile_start(g), tile < geo.tile_end(g)))
            def _(g=g):
                out_refs[g][...] = x3

    def stage_b():
        return _ffn(lambda: x2s_ref[...], res_ref, g2_ref, wg_ref, wu_ref, wd_ref, act_ref, finish)

    _two_stage(step, geo.n_tiles, stage_a, stage_b, order=MIX_OUT_ORDER)


def _cast_kernel(*refs):
    n = len(refs) // 2
    for src, dst in zip(refs[:n], refs[n:]):
        dst[...] = src[...].astype(_BF16)


def _cast_weights(stacks):
    specs = []
    for w in stacks:
        _, r, c = w.shape
        assert r % (CAST_STEPS * BF16_ROWS) == 0, w.shape
        specs.append(pl.BlockSpec((None, r // CAST_STEPS, c), lambda l, i: (l, i, 0)))
    return pl.pallas_call(
        _cast_kernel,
        grid=(DEPTH, CAST_STEPS),
        in_specs=specs,
        out_specs=specs,
        out_shape=[jax.ShapeDtypeStruct(w.shape, _BF16) for w in stacks],
        compiler_params=pltpu.CompilerParams(
            dimension_semantics=("parallel", "parallel"), vmem_limit_bytes=VMEM_LIMIT_BYTES),
        name="cast_weights",
    )(*stacks)


def _const_spec(shape):
    nd = len(shape)
    return pl.BlockSpec(shape, lambda *_: (0,) * nd, pipeline_mode=pl.Buffered(1))


def _layer_spec(layer, shape):
    nd = len(shape)
    return pl.BlockSpec((None,) + shape, lambda *_: (layer,) + (0,) * nd, pipeline_mode=pl.Buffered(1))


class _Geometry:
    def __init__(self, shapes):
        self.tiles = [b * s // TOKEN_TILE for b, s in shapes]
        self.per_seq = [s // TOKEN_TILE for _, s in shapes]
        assert all(s % TOKEN_TILE == 0 for _, s in shapes), shapes
        self.n_tiles = sum(self.tiles)
        self.rows = self.n_tiles * TOKEN_TILE

    def tile_start(self, g):
        return sum(self.tiles[:g])

    def tile_end(self, g):
        return sum(self.tiles[:g + 1])

    def tile_of(self, step):
        return jnp.minimum(step, self.n_tiles - 1)

    def group_tile(self, tile, g):
        return jnp.clip(tile - self.tile_start(g), 0, self.tiles[g] - 1)

    def seq_edges(self, tile):
        first = last = None
        for g in reversed(range(len(self.tiles))):
            pos = (tile - self.tile_start(g)) % self.per_seq[g]
            f, l = pos == 0, pos == self.per_seq[g] - 1
            if first is None:
                first, last = f, l
            else:
                inside = tile < self.tile_end(g)
                first, last = jnp.where(inside, f, first), jnp.where(inside, l, last)
        return first, last


def _mix_in(xs, layer, w, geo):
    tm = TOKEN_TILE
    n = geo.rows
    if len(xs) == 1:
        x_specs = [pl.BlockSpec((tm, D_MODEL), lambda s: (geo.tile_of(s), 0))]
    else:
        x_specs = [pl.BlockSpec((tm, D_MODEL), lambda s, g=g: (geo.group_tile(geo.tile_of(s), g), 0))
                   for g in range(len(xs))]
    cur = lambda width: pl.BlockSpec((tm, width), lambda s: (geo.tile_of(s), 0))
    behind = lambda width: pl.BlockSpec((tm, width), lambda s: (jnp.maximum(s - 1, 0), 0))
    return pl.pallas_call(
        functools.partial(_mix_in_kernel, geo=geo, n_inputs=len(xs)),
        grid=(geo.n_tiles + 1,),
        in_specs=x_specs + [
            _layer_spec(layer, (1, D_MODEL)),
            _layer_spec(layer, (D_MODEL, D_FF)),
            _layer_spec(layer, (D_MODEL, D_FF)),
            _layer_spec(layer, (D_FF, D_MODEL)),
            _layer_spec(layer, (1, D_MODEL)),
            _layer_spec(layer, (D_MODEL, IN_PROJ_WIDTH)),
            _layer_spec(layer, (1, GMLP_WIDTH)),
            _const_spec((GMLP_WIDTH, GMLP_WIDTH)),
            _layer_spec(layer, (N_GMLP_HEADS // 2, 2 * CHUNK, CHUNK)),
            _layer_spec(layer, (CHUNK, GMLP_WIDTH)),
        ],
        out_specs=[cur(D_MODEL), behind(ATTN_WIDTH), behind(KEXP_WIDTH), behind(KEXP_WIDTH),
                   behind(GMLP_WIDTH)],
        out_shape=[
            jax.ShapeDtypeStruct((n, D_MODEL), _F32),
            jax.ShapeDtypeStruct((n, ATTN_WIDTH), _BF16),
            jax.ShapeDtypeStruct((n, KEXP_WIDTH), _BF16),
            jax.ShapeDtypeStruct((n, KEXP_WIDTH), _BF16),
            jax.ShapeDtypeStruct((n, GMLP_WIDTH), _BF16),
        ],
        scratch_shapes=[pltpu.VMEM((tm, D_FF), _BF16), pltpu.VMEM((tm, D_MODEL), _F32)],
        compiler_params=pltpu.CompilerParams(
            dimension_semantics=("arbitrary",), vmem_limit_bytes=VMEM_LIMIT_BYTES),
        name="mix_in",
    )(*xs, w["g1"], w["wg1"], w["wu1"], w["wd1"], w["gm"], w["win"], w["vg"], w["msum"], w["wsp"],
      w["bsp"])


def _mix_out(x1, q, kx, vx, c, layer, w, geo, last_layer):
    tq = TOKEN_TILE
    r = tq // BLOCK

    def prev_block(s):
        tile = geo.tile_of(s)
        return (tile * r - jnp.where(geo.seq_edges(tile)[0], 0, 1), 0)

    def next_block(s):
        tile = geo.tile_of(s)
        return ((tile + 1) * r - jnp.where(geo.seq_edges(tile)[1], 1, 0), 0)

    edge = (None, N_ATTN_HEADS, BLOCK, BLOCK)
    tab_first = pl.BlockSpec(
        edge, lambda s: (jnp.where(geo.seq_edges(geo.tile_of(s))[0], 1, 0), 0, 0, 0))
    tab_inner = _const_spec((N_ATTN_HEADS, BLOCK, BAND))
    tab_last = pl.BlockSpec(
        edge, lambda s: (jnp.where(geo.seq_edges(geo.tile_of(s))[1], 1, 0), 0, 0, 0))

    cur = lambda width: pl.BlockSpec((tq, width), lambda s: (geo.tile_of(s), 0))
    prev = lambda width: pl.BlockSpec((BLOCK, width), prev_block)
    nxt = lambda width: pl.BlockSpec((BLOCK, width), next_block)
    if last_layer:
        out_specs = [pl.BlockSpec((tq, D_MODEL), lambda s, g=g: (geo.group_tile(s - 1, g), 0))
                     for g in range(len(geo.tiles))]
        out_shape = [jax.ShapeDtypeStruct((t * tq, D_MODEL), _F32) for t in geo.tiles]
    else:
        out_specs = [pl.BlockSpec((tq, D_MODEL), lambda s: (jnp.maximum(s - 1, 0), 0))]
        out_shape = [jax.ShapeDtypeStruct((geo.rows, D_MODEL), _F32)]
    return pl.pallas_call(
        functools.partial(_mix_out_kernel, geo=geo, layer=layer, n_outputs=len(out_specs)),
        grid=(geo.n_tiles + 1,),
        in_specs=[
            cur(D_MODEL), cur(ATTN_WIDTH),
            prev(KEXP_WIDTH), cur(KEXP_WIDTH), nxt(KEXP_WIDTH),
            prev(KEXP_WIDTH), cur(KEXP_WIDTH), nxt(KEXP_WIDTH),
            cur(GMLP_WIDTH),
            pl.BlockSpec(memory_space=pltpu.SMEM),
            tab_first, tab_inner, tab_last,
            _layer_spec(layer, (2 * ATTN_WIDTH, D_MODEL)),
            _layer_spec(layer, (1, D_MODEL)),
            _layer_spec(layer, (D_MODEL, D_FF)),
            _layer_spec(layer, (D_MODEL, D_FF)),
            _layer_spec(layer, (D_FF, D_MODEL)),
            _const_spec((1, D_MODEL)),
        ],
        out_specs=out_specs,
        out_shape=out_shape,
        scratch_shapes=[
            pltpu.VMEM((tq, 2 * ATTN_WIDTH), _BF16),
            pltpu.VMEM((tq, D_FF), _BF16),
            pltpu.VMEM((tq, D_MODEL), _F32),
            pltpu.VMEM((tq, D_MODEL), _F32),
        ],
        compiler_params=pltpu.CompilerParams(
            dimension_semantics=("arbitrary",), vmem_limit_bytes=VMEM_LIMIT_BYTES),
        name="mix_out",
    )(x1, q, kx, kx, kx, vx, vx, vx, c, w["sink"], w["tab_prev"], w["tab"], w["tab_next"], w["wo"], w["g2"],
      w["wg2"], w["wu2"], w["wd2"], w["gf"])


def _bias_tables():
    slopes = 2.0 ** (-8.0 * jnp.arange(1, N_ATTN_HEADS + 1, dtype=_F32) / N_ATTN_HEADS)
    qi = jnp.arange(BLOCK)[:, None]
    kj = jnp.arange(BAND)[None, :]
    rel = kj - BLOCK - qi
    dist = jnp.abs(rel).astype(_F32)
    alibi = -slopes[:, None, None] * dist[None]
    table = jnp.where((jnp.abs(rel) <= WINDOW)[None], alibi, -jnp.inf)
    absent = jnp.full((N_ATTN_HEADS, BLOCK, BLOCK), -jnp.inf, _F32)
    return (table, jnp.stack([table[:, :, :BLOCK], absent]),
            jnp.stack([table[:, :, 2 * BLOCK:], absent]))


def _prepare(norm_ffn1, w1_gate, w1_up, w1_down, norm_mix, w_in, sink, gmlp_v_gain, w_spatial,
             b_spatial, w_out, norm_ffn2, w2_gate, w2_up, w2_down, norm_final):
    wg1, wu1, wd1, win, wsp, wo, wg2, wu2, wd2 = _cast_weights([
        w1_gate, w1_up, w1_down, w_in,
        w_spatial.reshape(DEPTH, N_GMLP_HEADS * CHUNK, CHUNK),
        w_out, w2_gate, w2_up, w2_down])
    head_of_col = jnp.arange(GMLP_WIDTH) // HEAD_DIM
    tab, tab_prev, tab_next = _bias_tables()
    return {
        "g1": norm_ffn1[:, None, :], "wg1": wg1, "wu1": wu1, "wd1": wd1,
        "gm": norm_mix[:, None, :], "win": win,
        "vg": gmlp_v_gain[:, None, :],
        "msum": (head_of_col[:, None] == head_of_col[None, :]).astype(_BF16),
        "wsp": wsp.reshape(DEPTH, N_GMLP_HEADS // 2, 2 * CHUNK, CHUNK),
        "bsp": jnp.repeat(jnp.swapaxes(b_spatial, 1, 2), HEAD_DIM, axis=2),
        "sink": sink, "tab": tab, "tab_prev": tab_prev, "tab_next": tab_next, "wo": wo,
        "g2": norm_ffn2[:, None, :], "wg2": wg2, "wu2": wu2, "wd2": wd2,
        "gf": norm_final[None, :],
    }


def _forward(xs, w):
    geo = _Geometry([x.shape[:2] for x in xs])
    acts = [x.reshape(-1, D_MODEL) for x in xs]
    for layer in range(DEPTH):
        x1, q, kx, vx, c = _mix_in(acts, layer, w, geo)
        acts = _mix_out(x1, q, kx, vx, c, layer, w, geo, last_layer=(layer == DEPTH - 1))
    return tuple(y.reshape(x.shape) for x, y in zip(xs, acts))


def kernel(x_prompt, x_sample, norm_ffn1, w1_gate, w1_up, w1_down, norm_mix, w_in, sink, gmlp_v_gain, w_spatial, b_spatial, w_out, norm_ffn2, w2_gate, w2_up, w2_down, norm_final):
    w = _prepare(norm_ffn1, w1_gate, w1_up, w1_down, norm_mix, w_in, sink, gmlp_v_gain, w_spatial,
                 b_spatial, w_out, norm_ffn2, w2_gate, w2_up, w2_down, norm_final)
    return _forward((x_prompt, x_sample), w)
```

```python
import functools

import jax
import jax.numpy as jnp
from jax import lax
from jax.experimental import pallas as pl
from jax.experimental.pallas import tpu as pltpu

D_MODEL = 1024
DEPTH = 2
HEAD_DIM = 64
N_ATTN_HEADS = 8
N_KV_HEADS = 2
GROUP = N_ATTN_HEADS // N_KV_HEADS
ATTN_WIDTH = N_ATTN_HEADS * HEAD_DIM
KV_WIDTH = N_KV_HEADS * HEAD_DIM
N_GMLP_HEADS = 8
GMLP_WIDTH = N_GMLP_HEADS * HEAD_DIM
IN_PROJ_WIDTH = ATTN_WIDTH + 2 * KV_WIDTH + 2 * GMLP_WIDTH
WINDOW = 128
BLOCK = 128
CHUNK = 128
D_FF = 2816
EPS = 1e-6

LANES = 128
BF16_ROWS = 16
KEXP_WIDTH = 2 * N_KV_HEADS * LANES
BAND = 3 * BLOCK
TOKEN_TILE = 512
FF_CHUNK = 256
MIX_IN_ORDER = "ab" + "aaa" + "b" + "aaaa" + "b"
MIX_OUT_ORDER = "bb" + "ab" * 9
CAST_STEPS = 16
VMEM_LIMIT_BYTES = 58 * 1024 * 1024

_F32 = jnp.float32
_BF16 = jnp.bfloat16


def _rms(x, gain):
    ms = jnp.mean(x * x, axis=-1, keepdims=True)
    return x * lax.rsqrt(ms + EPS) * gain


def _ffn(load_x, res_ref, gain_ref, wg_ref, wu_ref, wd_ref, act_ref, finish):
    x = load_x()
    if res_ref is not None:
        res_ref[...] = x
    h = _rms(x, gain_ref[...]).astype(_BF16)
    for c in range(D_FF // FF_CHUNK):
        cols = slice(c * FF_CHUNK, (c + 1) * FF_CHUNK)
        g = jnp.dot(h, wg_ref[:, cols], preferred_element_type=_F32)
        u = jnp.dot(h, wu_ref[:, cols], preferred_element_type=_F32)
        act_ref[:, cols] = (jax.nn.silu(g) * u).astype(_BF16)
        yield
    y = jnp.dot(act_ref[...], wd_ref[...], preferred_element_type=_F32)
    finish((load_x() if res_ref is None else res_ref[...]) + 0.5 * y)


def _run(order, **stages):
    live = dict(stages)

    def advance(name):
        if name in live and next(live[name], StopIteration) is StopIteration:
            del live[name]

    for name in order:
        advance(name)
    while live:
        for name in list(live):
            advance(name)


def _two_stage(step, n_tiles, stage_a, stage_b, order):
    @pl.when(step == 0)
    def _():
        _run("", a=stage_a())

    @pl.when(jnp.logical_and(step > 0, step < n_tiles))
    def _():
        _run(order, a=stage_a(), b=stage_b())

    @pl.when(step == n_tiles)
    def _():
        _run("", b=stage_b())


def _gating_and_qkv(x1s_ref, gm_ref, win_ref, vg_ref, wsp_ref, bsp_ref,
                    q_ref, kx_ref, vx_ref, c_ref):
    tm = x1s_ref.shape[0]
    h2 = _rms(x1s_ref[...], gm_ref[...]).astype(_BF16)
    z = jnp.dot(h2, win_ref[...], preferred_element_type=_F32)
    yield

    o_k = ATTN_WIDTH
    o_v = o_k + KV_WIDTH
    o_u = o_v + KV_WIDTH
    o_g = o_u + GMLP_WIDTH
    q_ref[...] = (z[:, :o_k] * (HEAD_DIM ** -0.5)).astype(_BF16)
    zk = z[:, o_k:o_v]
    zr = pltpu.roll(zk, HEAD_DIM, 1)
    low = lax.broadcasted_iota(jnp.int32, (tm, LANES), 1) < HEAD_DIM
    zero = jnp.zeros_like(zk)
    kx_ref[:, 0 * LANES:1 * LANES] = jnp.where(low, zk, zero).astype(_BF16)
    kx_ref[:, 1 * LANES:2 * LANES] = jnp.where(low, zero, zr).astype(_BF16)
    kx_ref[:, 2 * LANES:3 * LANES] = jnp.where(low, zr, zero).astype(_BF16)
    kx_ref[:, 3 * LANES:4 * LANES] = jnp.where(low, zero, zk).astype(_BF16)
    zv = z[:, o_v:o_u]
    zvr = pltpu.roll(zv, HEAD_DIM, 1)
    one = jnp.ones_like(zv)
    vx_ref[:, 0 * LANES:1 * LANES] = jnp.where(low, zv, one).astype(_BF16)
    vx_ref[:, 1 * LANES:2 * LANES] = jnp.where(low, one, zvr).astype(_BF16)
    vx_ref[:, 2 * LANES:3 * LANES] = jnp.where(low, zvr, one).astype(_BF16)
    vx_ref[:, 3 * LANES:4 * LANES] = jnp.where(low, one, zv).astype(_BF16)

    u = jax.nn.gelu(z[:, o_u:o_g])
    gg = jax.nn.gelu(z[:, o_g:])
    sq = gg * gg
    lane = lax.broadcasted_iota(jnp.int32, (tm, LANES), 1)
    sums = []
    for p in range(GMLP_WIDTH // LANES):
        acc = sq[:, p * LANES:(p + 1) * LANES]
        k = HEAD_DIM // 2
        while k:
            partner = jnp.where((lane & k) != 0, pltpu.roll(acc, k, 1), pltpu.roll(acc, LANES - k, 1))
            acc = acc + partner
            k //= 2
        sums.append(acc)
    ssum = jnp.concatenate(sums, axis=1)
    gh = (gg * lax.rsqrt(ssum * (1.0 / HEAD_DIM) + EPS) * vg_ref[...]).astype(_BF16)
    yield

    low_c =lax.broadcasted_iota(jnp.int32, (CHUNK, LANES), 1) < HEAD_DIM
    n_chunks = tm // CHUNK
    for p in range(N_GMLP_HEADS // 2):
        c0 = p * LANES
        rhs = jnp.concatenate(
            [gh[ck * CHUNK:(ck + 1) * CHUNK, c0:c0 + LANES] for ck in range(n_chunks)], axis=1)
        mm = jnp.dot(wsp_ref[p], rhs, preferred_element_type=_F32)
        for ck in range(n_chunks):
            r0 = ck * CHUNK
            l0 = ck * LANES
            mixed = jnp.where(low_c, mm[:CHUNK, l0:l0 + LANES], mm[CHUNK:, l0:l0 + LANES])
            cc = u[r0:r0 + CHUNK, c0:c0 + LANES] * (mixed + bsp_ref[:, c0:c0 + LANES])
            c_ref[r0:r0 + CHUNK, c0:c0 + LANES] = cc.astype(_BF16)


def _mix_in_kernel(*refs, geo, n_inputs):
    x_refs = refs[:n_inputs]
    (g1_ref, wg_ref, wu_ref, wd_ref, gm_ref, win_ref, vg_ref, wsp_ref, bsp_ref,
     x1_ref, q_ref, kx_ref, vx_ref, c_ref, act_ref, x1s_ref) = refs[n_inputs:]
    step = pl.program_id(0)

    def finish(x1):
        x1_ref[...] = x1
        x1s_ref[...] = x1

    def load_x():
        x = x_refs[-1][...]
        for g in reversed(range(n_inputs - 1)):
            x = jnp.where(step < geo.tile_end(g), x_refs[g][...], x)
        return x

    def stage_a():
        return _ffn(load_x, None, g1_ref, wg_ref, wu_ref, wd_ref, act_ref, finish)

    def stage_b():
        return _gating_and_qkv(x1s_ref, gm_ref, win_ref, vg_ref, wsp_ref, bsp_ref,
                               q_ref, kx_ref, vx_ref, c_ref)

    _two_stage(step, geo.n_tiles, stage_a, stage_b, order=MIX_IN_ORDER)


def _attention(q_ref, kx_refs, vx_refs, sink_ref, tab_refs, ac_ref, layer):
    tq = q_ref.shape[0]
    nblk = tq // BLOCK
    assert nblk >= 2
    low = lax.broadcasted_iota(jnp.int32, (BLOCK, LANES), 1) < HEAD_DIM

    def band(refs, blk, lane_block):
        prev_ref, main_ref, next_ref = refs
        cols = slice(lane_block * LANES, (lane_block + 1) * LANES)
        lo = (blk - 1) * BLOCK
        hi = (blk + 2) * BLOCK
        pieces = []
        if lo < 0:
            pieces.append(prev_ref[:, cols])
        pieces.append(main_ref[max(lo, 0):min(hi, tq), cols])
        if hi > tq:
            pieces.append(next_ref[:, cols])
        return jnp.concatenate(pieces, axis=0)

    def scores(blk):
        r0 = blk * BLOCK
        qb = q_ref[r0:r0 + BLOCK, :]
        out = []
        for j in range(N_KV_HEADS):
            q0 = j * GROUP * HEAD_DIM
            lhs = jnp.concatenate([qb[:, q0:q0 + LANES], qb[:, q0 + LANES:q0 + 2 * LANES]], axis=0)
            kcat = jnp.concatenate([band(kx_refs, blk, 2 * j + hl) for hl in range(2)], axis=0)
            out.append(lax.dot_general(lhs, kcat, (((1,), (1,)), ((), ())),
                                       preferred_element_type=_F32))
        return out

    def softmax(blk, j, s):
        tabp_ref, tab_ref, tabn_ref = tab_refs

        def bias(hd):
            if blk == 0:
                return jnp.concatenate([tabp_ref[hd], tab_ref[hd, :, BLOCK:]], axis=1)
            if blk == nblk - 1:
                return jnp.concatenate([tab_ref[hd, :, :2 * BLOCK], tabn_ref[hd]], axis=1)
            return tab_ref[hd]

        probs = []
        sink_terms = []
        for hl in range(2):
            for e in range(2):
                hd = GROUP * j + 2 * e + hl
                se = s[e * BLOCK:(e + 1) * BLOCK, hl * BAND:(hl + 1) * BAND] + bias(hd)
                sk = sink_ref[layer, hd]
                m = jnp.maximum(jnp.max(se, axis=-1, keepdims=True), sk)
                probs.append(jnp.exp(se - m).astype(_BF16))
                sink_terms.append(jnp.exp(sk - m))
        return probs, sink_terms

    def weighted_values(blk, j, probs, sink_terms):
        r0 = blk * BLOCK
        pv_even = jnp.dot(jnp.concatenate(probs[:2], axis=0), band(vx_refs, blk, 2 * j),
                          preferred_element_type=_F32)
        pv_odd = jnp.dot(jnp.concatenate(probs[2:], axis=0), band(vx_refs, blk, 2 * j + 1),
                         preferred_element_type=_F32)
        for e in range(2):
            rows = slice(e * BLOCK, (e + 1) * BLOCK)
            pe = pv_even[rows]
            po = pv_odd[rows]
            denom = jnp.where(low, po + sink_terms[2 + e], pe + sink_terms[e])
            pair = jnp.where(low, pe, po) * pltpu.roll(1.0 / denom, HEAD_DIM, 1)
            c0 = (2 * j + e) * LANES
            ac_ref[r0:r0 + BLOCK, c0:c0 + LANES] = pair.astype(_BF16)

    ahead = 2
    sc = {blk: scores(blk) for blk in range(min(ahead, nblk))}
    yield
    ready = None
    for blk in range(nblk):
        for j in range(N_KV_HEADS):
            if j == 0 and blk + ahead < nblk:
                sc[blk + ahead] = scores(blk + ahead)
            fresh = (blk, j) + softmax(blk, j, sc[blk][j])
            if ready is not None:
                weighted_values(*ready)
            ready = fresh
            yield
    weighted_values(*ready)


def _mix_out_kernel(x1_ref, q_ref, kxp_ref, kxm_ref, kxn_ref, vxp_ref, vxm_ref, vxn_ref, c_ref,
                    sink_ref, tabf_ref, tabi_ref, tabl_ref, wo_ref, g2_ref, wg_ref, wu_ref, wd_ref,
                    gf_ref, *refs, geo, layer, n_outputs):
    out_refs = refs[:n_outputs]
    ac_ref, act_ref, x2s_ref, res_ref = refs[n_outputs:]
    step = pl.program_id(0)

    def stage_a():
        yield from _attention(q_ref, (kxp_ref, kxm_ref, kxn_ref), (vxp_ref, vxm_ref, vxn_ref),
                              sink_ref, (tabf_ref, tabi_ref, tabl_ref), ac_ref, layer)
        ac_ref[:, ATTN_WIDTH:] = c_ref[...]
        x2s_ref[...] = x1_ref[...] + jnp.dot(ac_ref[...], wo_ref[...], preferred_element_type=_F32)

    def finish(x3):
        if n_outputs == 1:
            out_refs[0][...] = x3
            return
        x3 = _rms(x3, gf_ref[...])
        tile = step - 1
        for g in range(n_outputs):
            @pl.when(jnp.logical_and(tile >= geo.tile_start(g), tile < geo.tile_end(g)))
            def _(g=g):
                out_refs[g][...] = x3

    def stage_b():
        return _ffn(lambda: x2s_ref[...], res_ref, g2_ref, wg_ref, wu_ref, wd_ref, act_ref, finish)

    _two_stage(step, geo.n_tiles, stage_a, stage_b, order=MIX_OUT_ORDER)


def _cast_kernel(*refs):
    n = len(refs) // 2
    for src, dst in zip(refs[:n], refs[n:]):
        dst[...] = src[...].astype(_BF16)


def _cast_weights(stacks):
    specs = []
    for w in stacks:
        _, r, c = w.shape
        assert r % (CAST_STEPS * BF16_ROWS) == 0, w.shape
        specs.append(pl.BlockSpec((None, r // CAST_STEPS, c), lambda l, i: (l, i, 0)))
    return pl.pallas_call(
        _cast_kernel,
        grid=(DEPTH, CAST_STEPS),
        in_specs=specs,
        out_specs=specs,
        out_shape=[jax.ShapeDtypeStruct(w.shape, _BF16) for w in stacks],
        compiler_params=pltpu.CompilerParams(
            dimension_semantics=("parallel", "parallel"), vmem_limit_bytes=VMEM_LIMIT_BYTES),
        name="cast_weights",
    )(*stacks)


def _const_spec(shape):
    nd = len(shape)
    return pl.BlockSpec(shape, lambda *_: (0,) * nd, pipeline_mode=pl.Buffered(1))


def _layer_spec(layer, shape):
    nd = len(shape)
    return pl.BlockSpec((None,) + shape, lambda *_: (layer,) + (0,) * nd, pipeline_mode=pl.Buffered(1))


class _Geometry:
    def __init__(self, shapes):
        self.tiles = [b * s // TOKEN_TILE for b, s in shapes]
        self.per_seq = [s // TOKEN_TILE for _, s in shapes]
        assert all(s % TOKEN_TILE == 0 for _, s in shapes), shapes
        self.n_tiles = sum(self.tiles)
        self.rows = self.n_tiles * TOKEN_TILE

    def tile_start(self, g):
        return sum(self.tiles[:g])

    def tile_end(self, g):
        return sum(self.tiles[:g + 1])

    def tile_of(self, step):
        return jnp.minimum(step, self.n_tiles - 1)

    def group_tile(self, tile, g):
        return jnp.clip(tile - self.tile_start(g), 0, self.tiles[g] - 1)

    def seq_edges(self, tile):
        first = last = None
        for g in reversed(range(len(self.tiles))):
            pos = (tile - self.tile_start(g)) % self.per_seq[g]
            f, l = pos == 0, pos == self.per_seq[g] - 1
            if first is None:
                first, last = f, l
            else:
                inside = tile < self.tile_end(g)
                first, last = jnp.where(inside, f, first), jnp.where(inside, l, last)
        return first, last


def _mix_in(xs, layer, w, geo):
    tm = TOKEN_TILE
    n = geo.rows
    if len(xs) == 1:
        x_specs = [pl.BlockSpec((tm, D_MODEL), lambda s: (geo.tile_of(s), 0))]
    else:
        x_specs = [pl.BlockSpec((tm, D_MODEL), lambda s, g=g: (geo.group_tile(geo.tile_of(s), g), 0))
                   for g in range(len(xs))]
    cur = lambda width: pl.BlockSpec((tm, width), lambda s: (geo.tile_of(s), 0))
    behind = lambda width: pl.BlockSpec((tm, width), lambda s: (jnp.maximum(s - 1, 0), 0))
    return pl.pallas_call(
        functools.partial(_mix_in_kernel, geo=geo, n_inputs=len(xs)),
        grid=(geo.n_tiles + 1,),
        in_specs=x_specs + [
            _layer_spec(layer, (1, D_MODEL)),
            _layer_spec(layer, (D_MODEL, D_FF)),
            _layer_spec(layer, (D_MODEL, D_FF)),
            _layer_spec(layer, (D_FF, D_MODEL)),
            _layer_spec(layer, (1, D_MODEL)),
            _layer_spec(layer, (D_MODEL, IN_PROJ_WIDTH)),
            _layer_spec(layer, (1, GMLP_WIDTH)),
            _layer_spec(layer, (N_GMLP_HEADS // 2, 2 * CHUNK, CHUNK)),
            _layer_spec(layer, (CHUNK, GMLP_WIDTH)),
        ],
        out_specs=[cur(D_MODEL), behind(ATTN_WIDTH), behind(KEXP_WIDTH), behind(KEXP_WIDTH),
                   behind(GMLP_WIDTH)],
        out_shape=[
            jax.ShapeDtypeStruct((n, D_MODEL), _F32),
            jax.ShapeDtypeStruct((n, ATTN_WIDTH), _BF16),
            jax.ShapeDtypeStruct((n, KEXP_WIDTH), _BF16),
            jax.ShapeDtypeStruct((n, KEXP_WIDTH), _BF16),
            jax.ShapeDtypeStruct((n, GMLP_WIDTH), _BF16),
        ],
        scratch_shapes=[pltpu.VMEM((tm, D_FF), _BF16), pltpu.VMEM((tm, D_MODEL), _F32)],
        compiler_params=pltpu.CompilerParams(
            dimension_semantics=("arbitrary",), vmem_limit_bytes=VMEM_LIMIT_BYTES),
        name="mix_in",
    )(*xs, w["g1"], w["wg1"], w["wu1"], w["wd1"], w["gm"], w["win"], w["vg"], w["wsp"],
      w["bsp"])


def _mix_out(x1, q, kx, vx, c, layer, w, geo, last_layer):
    tq = TOKEN_TILE
    r = tq // BLOCK

    def prev_block(s):
        tile = geo.tile_of(s)
        return (tile * r - jnp.where(geo.seq_edges(tile)[0], 0, 1), 0)

    def next_block(s):
        tile = geo.tile_of(s)
        return ((tile + 1) * r - jnp.where(geo.seq_edges(tile)[1], 1, 0), 0)

    edge = (None, N_ATTN_HEADS, BLOCK, BLOCK)
    tab_first = pl.BlockSpec(
        edge, lambda s: (jnp.where(geo.seq_edges(geo.tile_of(s))[0], 1, 0), 0, 0, 0))
    tab_inner = _const_spec((N_ATTN_HEADS, BLOCK, BAND))
    tab_last = pl.BlockSpec(
        edge, lambda s: (jnp.where(geo.seq_edges(geo.tile_of(s))[1], 1, 0), 0, 0, 0))

    cur = lambda width: pl.BlockSpec((tq, width), lambda s: (geo.tile_of(s), 0))
    prev = lambda width: pl.BlockSpec((BLOCK, width), prev_block)
    nxt = lambda width: pl.BlockSpec((BLOCK, width), next_block)
    if last_layer:
        out_specs = [pl.BlockSpec((tq, D_MODEL), lambda s, g=g: (geo.group_tile(s - 1, g), 0))
                     for g in range(len(geo.tiles))]
        out_shape = [jax.ShapeDtypeStruct((t * tq, D_MODEL), _F32) for t in geo.tiles]
    else:
        out_specs = [pl.BlockSpec((tq, D_MODEL), lambda s: (jnp.maximum(s - 1, 0), 0))]
        out_shape = [jax.ShapeDtypeStruct((geo.rows, D_MODEL), _F32)]
    return pl.pallas_call(
        functools.partial(_mix_out_kernel, geo=geo, layer=layer, n_outputs=len(out_specs)),
        grid=(geo.n_tiles + 1,),
        in_specs=[
            cur(D_MODEL), cur(ATTN_WIDTH),
            prev(KEXP_WIDTH), cur(KEXP_WIDTH), nxt(KEXP_WIDTH),
            prev(KEXP_WIDTH), cur(KEXP_WIDTH), nxt(KEXP_WIDTH),
            cur(GMLP_WIDTH),
            pl.BlockSpec(memory_space=pltpu.SMEM),
            tab_first, tab_inner, tab_last,
            _layer_spec(layer, (2 * ATTN_WIDTH, D_MODEL)),
            _layer_spec(layer, (1, D_MODEL)),
            _layer_spec(layer, (D_MODEL, D_FF)),
            _layer_spec(layer, (D_MODEL, D_FF)),
            _layer_spec(layer, (D_FF, D_MODEL)),
            _const_spec((1, D_MODEL)),
        ],
        out_specs=out_specs,
        out_shape=out_shape,
        scratch_shapes=[
            pltpu.VMEM((tq, 2 * ATTN_WIDTH), _BF16),
            pltpu.VMEM((tq, D_FF), _BF16),
            pltpu.VMEM((tq, D_MODEL), _F32),
            pltpu.VMEM((tq, D_MODEL), _F32),
        ],
        compiler_params=pltpu.CompilerParams(
            dimension_semantics=("arbitrary",), vmem_limit_bytes=VMEM_LIMIT_BYTES),
        name="mix_out",
    )(x1, q, kx, kx, kx, vx, vx, vx, c, w["sink"], w["tab_prev"], w["tab"], w["tab_next"], w["wo"], w["g2"],
      w["wg2"], w["wu2"], w["wd2"], w["gf"])


def _bias_tables():
    slopes = 2.0 ** (-8.0 * jnp.arange(1, N_ATTN_HEADS + 1, dtype=_F32) / N_ATTN_HEADS)
    qi = jnp.arange(BLOCK)[:, None]
    kj = jnp.arange(BAND)[None, :]
    rel = kj - BLOCK - qi
    dist = jnp.abs(rel).astype(_F32)
    alibi = -slopes[:, None, None] * dist[None]
    table = jnp.where((jnp.abs(rel) <= WINDOW)[None], alibi, -jnp.inf)
    absent = jnp.full((N_ATTN_HEADS, BLOCK, BLOCK), -jnp.inf, _F32)
    return (table, jnp.stack([table[:, :, :BLOCK], absent]),
            jnp.stack([table[:, :, 2 * BLOCK:], absent]))


def _prepare(norm_ffn1, w1_gate, w1_up, w1_down, norm_mix, w_in, sink, gmlp_v_gain, w_spatial,
             b_spatial, w_out, norm_ffn2, w2_gate, w2_up, w2_down, norm_final):
    wg1, wu1, wd1, win, wsp, wo, wg2, wu2, wd2 = _cast_weights([
        w1_gate, w1_up, w1_down, w_in,
        w_spatial.reshape(DEPTH, N_GMLP_HEADS * CHUNK, CHUNK),
        w_out, w2_gate, w2_up, w2_down])
    tab, tab_prev, tab_next = _bias_tables()
    return {
        "g1": norm_ffn1[:, None, :], "wg1": wg1, "wu1": wu1, "wd1": wd1,
        "gm": norm_mix[:, None, :], "win": win,
        "vg": gmlp_v_gain[:, None, :],
        "wsp": wsp.reshape(DEPTH, N_GMLP_HEADS // 2, 2 * CHUNK, CHUNK),
        "bsp": jnp.repeat(jnp.swapaxes(b_spatial, 1, 2), HEAD_DIM, axis=2),
        "sink": sink, "tab": tab, "tab_prev": tab_prev, "tab_next": tab_next, "wo": wo,
        "g2": norm_ffn2[:, None, :], "wg2": wg2, "wu2": wu2, "wd2": wd2,
        "gf": norm_final[None, :],
    }


def _forward(xs, w):
    geo = _Geometry([x.shape[:2] for x in xs])
    acts = [x.reshape(-1, D_MODEL) for x in xs]
    for layer in range(DEPTH):
        x1, q, kx, vx, c = _mix_in(acts, layer, w, geo)
        acts = _mix_out(x1, q, kx, vx, c, layer, w, geo, last_layer=(layer == DEPTH - 1))
    return tuple(y.reshape(x.shape) for x, y in zip(xs, acts))


def kernel(x_prompt, x_sample, norm_ffn1, w1_gate, w1_up, w1_down, norm_mix, w_in, sink, gmlp_v_gain, w_spatial, b_spatial, w_out, norm_ffn2, w2_gate, w2_up, w2_down, norm_final):
    w = _prepare(norm_ffn1, w1_gate, w1_up, w1_down, norm_mix, w_in, sink, gmlp_v_gain, w_spatial,
                 b_spatial, w_out, norm_ffn2, w2_gate, w2_up, w2_down, norm_final)
    return _forward((x_prompt, x_sample), w)
```

```python
import functools

import jax
import jax.numpy as jnp
from jax import lax
from jax.experimental import pallas as pl
from jax.experimental.pallas import tpu as pltpu

D_MODEL = 1024
DEPTH = 2
HEAD_DIM = 64
N_ATTN_HEADS = 8
N_KV_HEADS = 2
GROUP = N_ATTN_HEADS // N_KV_HEADS
ATTN_WIDTH = N_ATTN_HEADS * HEAD_DIM
KV_WIDTH = N_KV_HEADS * HEAD_DIM
N_GMLP_HEADS = 8
GMLP_WIDTH = N_GMLP_HEADS * HEAD_DIM
IN_PROJ_WIDTH = ATTN_WIDTH + 2 * KV_WIDTH + 2 * GMLP_WIDTH
WINDOW = 128
BLOCK = 128
CHUNK = 128
D_FF = 2816
EPS = 1e-6

LANES = 128
BF16_ROWS = 16
KEXP_WIDTH = 2 * N_KV_HEADS * LANES
BAND = 3 * BLOCK
TOKEN_TILE = 512
FF_CHUNK_IN = 512
FF_CHUNK_OUT = 256
MIX_IN_ORDER = "ab" + "aa" + "b" + "aa" + "b"
MIX_OUT_ORDER = "bb" + "ab" * 9
CAST_STEPS = 16
VMEM_LIMIT_BYTES = 58 * 1024 * 1024

_F32 = jnp.float32
_BF16 = jnp.bfloat16


def _rms(x, gain):
    ms = jnp.mean(x * x, axis=-1, keepdims=True)
    return x * lax.rsqrt(ms + EPS) * gain


def _ffn(load_x, res_ref, gain_ref, wg_ref, wu_ref, wd_ref, act_ref, finish, chunk):
    x = load_x()
    if res_ref is not None:
        res_ref[...] = x
    h = _rms(x, gain_ref[...]).astype(_BF16)
    for c0 in range(0, D_FF, chunk):
        cols = slice(c0, min(c0 + chunk, D_FF))
        g = jnp.dot(h, wg_ref[:, cols], preferred_element_type=_F32)
        u = jnp.dot(h, wu_ref[:, cols], preferred_element_type=_F32)
        act_ref[:, cols] = (jax.nn.silu(g) * u).astype(_BF16)
        yield
    y = jnp.dot(act_ref[...], wd_ref[...], preferred_element_type=_F32)
    finish((load_x() if res_ref is None else res_ref[...]) + 0.5 * y)


def _run(order, **stages):
    live = dict(stages)

    def advance(name):
        if name in live and next(live[name], StopIteration) is StopIteration:
            del live[name]

    for name in order:
        advance(name)
    while live:
        for name in list(live):
            advance(name)


def _two_stage(step, n_tiles, stage_a, stage_b, order):
    @pl.when(step == 0)
    def _():
        _run("", a=stage_a())

    @pl.when(jnp.logical_and(step > 0, step < n_tiles))
    def _():
        _run(order, a=stage_a(), b=stage_b())

    @pl.when(step == n_tiles)
    def _():
        _run("", b=stage_b())


def _gating_and_qkv(x1s_ref, gm_ref, win_ref, vg_ref, wsp_ref, bsp_ref,
                    q_ref, kx_ref, vx_ref, c_ref):
    tm = x1s_ref.shape[0]
    h2 = _rms(x1s_ref[...], gm_ref[...]).astype(_BF16)
    z = jnp.dot(h2, win_ref[...], preferred_element_type=_F32)
    yield

    o_k = ATTN_WIDTH
    o_v = o_k + KV_WIDTH
    o_u = o_v + KV_WIDTH
    o_g = o_u + GMLP_WIDTH
    q_ref[...] = (z[:, :o_k] * (HEAD_DIM ** -0.5)).astype(_BF16)
    zk = z[:, o_k:o_v]
    zr = pltpu.roll(zk, HEAD_DIM, 1)
    low = lax.broadcasted_iota(jnp.int32, (tm, LANES), 1) < HEAD_DIM
    zero = jnp.zeros_like(zk)
    kx_ref[:, 0 * LANES:1 * LANES] = jnp.where(low, zk, zero).astype(_BF16)
    kx_ref[:, 1 * LANES:2 * LANES] = jnp.where(low, zero, zr).astype(_BF16)
    kx_ref[:, 2 * LANES:3 * LANES] = jnp.where(low, zr, zero).astype(_BF16)
    kx_ref[:, 3 * LANES:4 * LANES] = jnp.where(low, zero, zk).astype(_BF16)
    zv = z[:, o_v:o_u]
    zvr = pltpu.roll(zv, HEAD_DIM, 1)
    one = jnp.ones_like(zv)
    vx_ref[:, 0 * LANES:1 * LANES] = jnp.where(low, zv, one).astype(_BF16)
    vx_ref[:, 1 * LANES:2 * LANES] = jnp.where(low, one, zvr).astype(_BF16)
    vx_ref[:, 2 * LANES:3 * LANES] = jnp.where(low, zvr, one).astype(_BF16)
    vx_ref[:, 3 * LANES:4 * LANES] = jnp.where(low, one, zv).astype(_BF16)

    u = jax.nn.gelu(z[:, o_u:o_g])
    gg = jax.nn.gelu(z[:, o_g:])
    sq = gg * gg
    lane = lax.broadcasted_iota(jnp.int32, (tm, LANES), 1)
    sums = []
    for p in range(GMLP_WIDTH // LANES):
        acc = sq[:, p * LANES:(p + 1) * LANES]
        k = HEAD_DIM // 2
        while k:
            partner = jnp.where((lane & k) != 0, pltpu.roll(acc, k, 1), pltpu.roll(acc, LANES - k, 1))
            acc = acc + partner
            k //= 2
        sums.append(acc)
    ssum = jnp.concatenate(sums, axis=1)
    gh = (gg * lax.rsqrt(ssum * (1.0 / HEAD_DIM) + EPS) * vg_ref[...]).astype(_BF16)
    yield

    low_c =lax.broadcasted_iota(jnp.int32, (CHUNK, LANES), 1) < HEAD_DIM
    n_chunks = tm // CHUNK
    for p in range(N_GMLP_HEADS // 2):
        c0 = p * LANES
        rhs = jnp.concatenate(
            [gh[ck * CHUNK:(ck + 1) * CHUNK, c0:c0 + LANES] for ck in range(n_chunks)], axis=1)
        mm = jnp.dot(wsp_ref[p], rhs, preferred_element_type=_F32)
        for ck in range(n_chunks):
            r0 = ck * CHUNK
            l0 = ck * LANES
            mixed = jnp.where(low_c, mm[:CHUNK, l0:l0 + LANES], mm[CHUNK:, l0:l0 + LANES])
            cc = u[r0:r0 + CHUNK, c0:c0 + LANES] * (mixed + bsp_ref[:, c0:c0 + LANES])
            c_ref[r0:r0 + CHUNK, c0:c0 + LANES] = cc.astype(_BF16)


def _mix_in_kernel(*refs, geo, n_inputs):
    x_refs = refs[:n_inputs]
    (g1_ref, wg_ref, wu_ref, wd_ref, gm_ref, win_ref, vg_ref, wsp_ref, bsp_ref,
     x1_ref, q_ref, kx_ref, vx_ref, c_ref, act_ref, x1s_ref) = refs[n_inputs:]
    step = pl.program_id(0)

    def finish(x1):
        x1_ref[...] = x1
        x1s_ref[...] = x1

    def load_x():
        x = x_refs[-1][...]
        for g in reversed(range(n_inputs - 1)):
            x = jnp.where(step < geo.tile_end(g), x_refs[g][...], x)
        return x

    def stage_a():
        return _ffn(load_x, None, g1_ref, wg_ref, wu_ref, wd_ref, act_ref, finish, FF_CHUNK_IN)

    def stage_b():
        return _gating_and_qkv(x1s_ref, gm_ref, win_ref, vg_ref, wsp_ref, bsp_ref,
                               q_ref, kx_ref, vx_ref, c_ref)

    _two_stage(step, geo.n_tiles, stage_a, stage_b, order=MIX_IN_ORDER)


def _attention(q_ref, kx_refs, vx_refs, sink_ref, tab_refs, ac_ref, layer):
    tq = q_ref.shape[0]
    nblk = tq // BLOCK
    assert nblk >= 2
    low = lax.broadcasted_iota(jnp.int32, (BLOCK, LANES), 1) < HEAD_DIM

    def band(refs, blk, lane_block):
        prev_ref, main_ref, next_ref = refs
        cols = slice(lane_block * LANES, (lane_block + 1) * LANES)
        lo = (blk - 1) * BLOCK
        hi = (blk + 2) * BLOCK
        pieces = []
        if lo < 0:
            pieces.append(prev_ref[:, cols])
        pieces.append(main_ref[max(lo, 0):min(hi, tq), cols])
        if hi > tq:
            pieces.append(next_ref[:, cols])
        return jnp.concatenate(pieces, axis=0)

    def scores(blk):
        r0 = blk * BLOCK
        qb = q_ref[r0:r0 + BLOCK, :]
        out = []
        for j in range(N_KV_HEADS):
            q0 = j * GROUP * HEAD_DIM
            lhs = jnp.concatenate([qb[:, q0:q0 + LANES], qb[:, q0 + LANES:q0 + 2 * LANES]], axis=0)
            kcat = jnp.concatenate([band(kx_refs, blk, 2 * j + hl) for hl in range(2)], axis=0)
            out.append(lax.dot_general(lhs, kcat, (((1,), (1,)), ((), ())),
                                       preferred_element_type=_F32))
        return out

    def softmax(blk, j, s):
        tabp_ref, tab_ref, tabn_ref = tab_refs

        def bias(hd):
            if blk == 0:
                return jnp.concatenate([tabp_ref[hd], tab_ref[hd, :, BLOCK:]], axis=1)
            if blk == nblk - 1:
                return jnp.concatenate([tab_ref[hd, :, :2 * BLOCK], tabn_ref[hd]], axis=1)
            return tab_ref[hd]

        probs = []
        sink_terms = []
        for hl in range(2):
            for e in range(2):
                hd = GROUP * j + 2 * e + hl
                se = s[e * BLOCK:(e + 1) * BLOCK, hl * BAND:(hl + 1) * BAND] + bias(hd)
                sk = sink_ref[layer, hd]
                m = jnp.maximum(jnp.max(se, axis=-1, keepdims=True), sk)
                probs.append(jnp.exp(se - m).astype(_BF16))
                sink_terms.append(jnp.exp(sk - m))
        return probs, sink_terms

    def weighted_values(blk, j, probs, sink_terms):
        r0 = blk * BLOCK
        pv_even = jnp.dot(jnp.concatenate(probs[:2], axis=0), band(vx_refs, blk, 2 * j),
                          preferred_element_type=_F32)
        pv_odd = jnp.dot(jnp.concatenate(probs[2:], axis=0), band(vx_refs, blk, 2 * j + 1),
                         preferred_element_type=_F32)
        for e in range(2):
            rows = slice(e * BLOCK, (e + 1) * BLOCK)
            pe = pv_even[rows]
            po = pv_odd[rows]
            denom = jnp.where(low, po + sink_terms[2 + e], pe + sink_terms[e])
            pair = jnp.where(low, pe, po) * pltpu.roll(1.0 / denom, HEAD_DIM, 1)
            c0 = (2 * j + e) * LANES
            ac_ref[r0:r0 + BLOCK, c0:c0 + LANES] = pair.astype(_BF16)

    ahead = 2
    sc = {blk: scores(blk) for blk in range(min(ahead, nblk))}
    yield
    ready = None
    for blk in range(nblk):
        for j in range(N_KV_HEADS):
            if j == 0 and blk + ahead < nblk:
                sc[blk + ahead] = scores(blk + ahead)
            fresh = (blk, j) + softmax(blk, j, sc[blk][j])
            if ready is not None:
                weighted_values(*ready)
            ready = fresh
            yield
    weighted_values(*ready)


def _mix_out_kernel(x1_ref, q_ref, kxp_ref, kxm_ref, kxn_ref, vxp_ref, vxm_ref, vxn_ref, c_ref,
                    sink_ref, tabf_ref, tabi_ref, tabl_ref, wo_ref, g2_ref, wg_ref, wu_ref, wd_ref,
                    gf_ref, *refs, geo, layer, n_outputs):
    out_refs = refs[:n_outputs]
    ac_ref, act_ref, x2s_ref, res_ref = refs[n_outputs:]
    step = pl.program_id(0)

    def stage_a():
        yield from _attention(q_ref, (kxp_ref, kxm_ref, kxn_ref), (vxp_ref, vxm_ref, vxn_ref),
                              sink_ref, (tabf_ref, tabi_ref, tabl_ref), ac_ref, layer)
        ac_ref[:, ATTN_WIDTH:] = c_ref[...]
        x2s_ref[...] = x1_ref[...] + jnp.dot(ac_ref[...], wo_ref[...], preferred_element_type=_F32)

    def finish(x3):
        if n_outputs == 1:
            out_refs[0][...] = x3
            return
        x3 = _rms(x3, gf_ref[...])
        tile = step - 1
        for g in range(n_outputs):
            @pl.when(jnp.logical_and(tile >= geo.tile_start(g), tile < geo.tile_end(g)))
            def _(g=g):
                out_refs[g][...] = x3

    def stage_b():
        return _ffn(lambda: x2s_ref[...], res_ref, g2_ref, wg_ref, wu_ref, wd_ref, act_ref, finish,
                    FF_CHUNK_OUT)

    _two_stage(step, geo.n_tiles, stage_a, stage_b, order=MIX_OUT_ORDER)


def _cast_kernel(*refs):
    n = len(refs) // 2
    for src, dst in zip(refs[:n], refs[n:]):
        dst[...] = src[...].astype(_BF16)


def _cast_weights(stacks):
    specs = []
    for w in stacks:
        _, r, c = w.shape
        assert r % (CAST_STEPS * BF16_ROWS) == 0, w.shape
        specs.append(pl.BlockSpec((None, r // CAST_STEPS, c), lambda l, i: (l, i, 0)))
    return pl.pallas_call(
        _cast_kernel,
        grid=(DEPTH, CAST_STEPS),
        in_specs=specs,
        out_specs=specs,
        out_shape=[jax.ShapeDtypeStruct(w.shape, _BF16) for w in stacks],
        compiler_params=pltpu.CompilerParams(
            dimension_semantics=("parallel", "parallel"), vmem_limit_bytes=VMEM_LIMIT_BYTES),
        name="cast_weights",
    )(*stacks)


def _const_spec(shape):
    nd = len(shape)
    return pl.BlockSpec(shape, lambda *_: (0,) * nd, pipeline_mode=pl.Buffered(1))


def _layer_spec(layer, shape):
    nd = len(shape)
    return pl.BlockSpec((None,) + shape, lambda *_: (layer,) + (0,) * nd, pipeline_mode=pl.Buffered(1))


class _Geometry:
    def __init__(self, shapes):
        self.tiles = [b * s // TOKEN_TILE for b, s in shapes]
        self.per_seq = [s // TOKEN_TILE for _, s in shapes]
        assert all(s % TOKEN_TILE == 0 for _, s in shapes), shapes
        self.n_tiles = sum(self.tiles)
        self.rows = self.n_tiles * TOKEN_TILE

    def tile_start(self, g):
        return sum(self.tiles[:g])

    def tile_end(self, g):
        return sum(self.tiles[:g + 1])

    def tile_of(self, step):
        return jnp.minimum(step, self.n_tiles - 1)

    def group_tile(self, tile, g):
        return jnp.clip(tile - self.tile_start(g), 0, self.tiles[g] - 1)

    def seq_edges(self, tile):
        first = last = None
        for g in reversed(range(len(self.tiles))):
            pos = (tile - self.tile_start(g)) % self.per_seq[g]
            f, l = pos == 0, pos == self.per_seq[g] - 1
            if first is None:
                first, last = f, l
            else:
                inside = tile < self.tile_end(g)
                first, last = jnp.where(inside, f, first), jnp.where(inside, l, last)
        return first, last


def _mix_in(xs, layer, w, geo):
    tm = TOKEN_TILE
    n = geo.rows
    if len(xs) == 1:
        x_specs = [pl.BlockSpec((tm, D_MODEL), lambda s: (geo.tile_of(s), 0))]
    else:
        x_specs = [pl.BlockSpec((tm, D_MODEL), lambda s, g=g: (geo.group_tile(geo.tile_of(s), g), 0))
                   for g in range(len(xs))]
    cur = lambda width: pl.BlockSpec((tm, width), lambda s: (geo.tile_of(s), 0))
    behind = lambda width: pl.BlockSpec((tm, width), lambda s: (jnp.maximum(s - 1, 0), 0))
    return pl.pallas_call(
        functools.partial(_mix_in_kernel, geo=geo, n_inputs=len(xs)),
        grid=(geo.n_tiles + 1,),
        in_specs=x_specs + [
            _layer_spec(layer, (1, D_MODEL)),
            _layer_spec(layer, (D_MODEL, D_FF)),
            _layer_spec(layer, (D_MODEL, D_FF)),
            _layer_spec(layer, (D_FF, D_MODEL)),
            _layer_spec(layer, (1, D_MODEL)),
            _layer_spec(layer, (D_MODEL, IN_PROJ_WIDTH)),
            _layer_spec(layer, (1, GMLP_WIDTH)),
            _layer_spec(layer, (N_GMLP_HEADS // 2, 2 * CHUNK, CHUNK)),
            _layer_spec(layer, (CHUNK, GMLP_WIDTH)),
        ],
        out_specs=[cur(D_MODEL), behind(ATTN_WIDTH), behind(KEXP_WIDTH), behind(KEXP_WIDTH),
                   behind(GMLP_WIDTH)],
        out_shape=[
            jax.ShapeDtypeStruct((n, D_MODEL), _F32),
            jax.ShapeDtypeStruct((n, ATTN_WIDTH), _BF16),
            jax.ShapeDtypeStruct((n, KEXP_WIDTH), _BF16),
            jax.ShapeDtypeStruct((n, KEXP_WIDTH), _BF16),
            jax.ShapeDtypeStruct((n, GMLP_WIDTH), _BF16),
        ],
        scratch_shapes=[pltpu.VMEM((tm, D_FF), _BF16), pltpu.VMEM((tm, D_MODEL), _F32)],
        compiler_params=pltpu.CompilerParams(
            dimension_semantics=("arbitrary",), vmem_limit_bytes=VMEM_LIMIT_BYTES),
        name="mix_in",
    )(*xs, w["g1"], w["wg1"], w["wu1"], w["wd1"], w["gm"], w["win"], w["vg"], w["wsp"],
      w["bsp"])


def _mix_out(x1, q, kx, vx, c, layer, w, geo, last_layer):
    tq = TOKEN_TILE
    r = tq // BLOCK

    def prev_block(s):
        tile = geo.tile_of(s)
        return (tile * r - jnp.where(geo.seq_edges(tile)[0], 0, 1), 0)

    def next_block(s):
        tile = geo.tile_of(s)
        return ((tile + 1) * r - jnp.where(geo.seq_edges(tile)[1], 1, 0), 0)

    edge = (None, N_ATTN_HEADS, BLOCK, BLOCK)
    tab_first = pl.BlockSpec(
        edge, lambda s: (jnp.where(geo.seq_edges(geo.tile_of(s))[0], 1, 0), 0, 0, 0))
    tab_inner = _const_spec((N_ATTN_HEADS, BLOCK, BAND))
    tab_last = pl.BlockSpec(
        edge, lambda s: (jnp.where(geo.seq_edges(geo.tile_of(s))[1], 1, 0), 0, 0, 0))

    cur = lambda width: pl.BlockSpec((tq, width), lambda s: (geo.tile_of(s), 0))
    prev = lambda width: pl.BlockSpec((BLOCK, width), prev_block)
    nxt = lambda width: pl.BlockSpec((BLOCK, width), next_block)
    if last_layer:
        out_specs = [pl.BlockSpec((tq, D_MODEL), lambda s, g=g: (geo.group_tile(s - 1, g), 0))
                     for g in range(len(geo.tiles))]
        out_shape = [jax.ShapeDtypeStruct((t * tq, D_MODEL), _F32) for t in geo.tiles]
    else:
        out_specs = [pl.BlockSpec((tq, D_MODEL), lambda s: (jnp.maximum(s - 1, 0), 0))]
        out_shape = [jax.ShapeDtypeStruct((geo.rows, D_MODEL), _F32)]
    return pl.pallas_call(
        functools.partial(_mix_out_kernel, geo=geo, layer=layer, n_outputs=len(out_specs)),
        grid=(geo.n_tiles + 1,),
        in_specs=[
            cur(D_MODEL), cur(ATTN_WIDTH),
            prev(KEXP_WIDTH), cur(KEXP_WIDTH), nxt(KEXP_WIDTH),
            prev(KEXP_WIDTH), cur(KEXP_WIDTH), nxt(KEXP_WIDTH),
            cur(GMLP_WIDTH),
            pl.BlockSpec(memory_space=pltpu.SMEM),
            tab_first, tab_inner, tab_last,
            _layer_spec(layer, (2 * ATTN_WIDTH, D_MODEL)),
            _layer_spec(layer, (1, D_MODEL)),
            _layer_spec(layer, (D_MODEL, D_FF)),
            _layer_spec(layer, (D_MODEL, D_FF)),
            _layer_spec(layer, (D_FF, D_MODEL)),
            _const_spec((1, D_MODEL)),
        ],
        out_specs=out_specs,
        out_shape=out_shape,
        scratch_shapes=[
            pltpu.VMEM((tq, 2 * ATTN_WIDTH), _BF16),
            pltpu.VMEM((tq, D_FF), _BF16),
            pltpu.VMEM((tq, D_MODEL), _F32),
            pltpu.VMEM((tq, D_MODEL), _F32),
        ],
        compiler_params=pltpu.CompilerParams(
            dimension_semantics=("arbitrary",), vmem_limit_bytes=VMEM_LIMIT_BYTES),
        name="mix_out",
    )(x1, q, kx, kx, kx, vx, vx, vx, c, w["sink"], w["tab_prev"], w["tab"], w["tab_next"], w["wo"], w["g2"],
      w["wg2"], w["wu2"], w["wd2"], w["gf"])


def _bias_tables():
    slopes = 2.0 ** (-8.0 * jnp.arange(1, N_ATTN_HEADS + 1, dtype=_F32) / N_ATTN_HEADS)
    qi = jnp.arange(BLOCK)[:, None]
    kj = jnp.arange(BAND)[None, :]
    rel = kj - BLOCK - qi
    dist = jnp.abs(rel).astype(_F32)
    alibi = -slopes[:, None, None] * dist[None]
    table = jnp.where((jnp.abs(rel) <= WINDOW)[None], alibi, -jnp.inf)
    absent = jnp.full((N_ATTN_HEADS, BLOCK, BLOCK), -jnp.inf, _F32)
    return (table, jnp.stack([table[:, :, :BLOCK], absent]),
            jnp.stack([table[:, :, 2 * BLOCK:], absent]))


def _prepare(norm_ffn1, w1_gate, w1_up, w1_down, norm_mix, w_in, sink, gmlp_v_gain, w_spatial,
             b_spatial, w_out, norm_ffn2, w2_gate, w2_up, w2_down, norm_final):
    wg1, wu1, wd1, win, wsp, wo, wg2, wu2, wd2 = _cast_weights([
        w1_gate, w1_up, w1_down, w_in,
        w_spatial.reshape(DEPTH, N_GMLP_HEADS * CHUNK, CHUNK),
        w_out, w2_gate, w2_up, w2_down])
    tab, tab_prev, tab_next = _bias_tables()
    return {
        "g1": norm_ffn1[:, None, :], "wg1": wg1, "wu1": wu1, "wd1": wd1,
        "gm": norm_mix[:, None, :], "win": win,
        "vg": gmlp_v_gain[:, None, :],
        "wsp": wsp.reshape(DEPTH, N_GMLP_HEADS // 2, 2 * CHUNK, CHUNK),
        "bsp": jnp.repeat(jnp.swapaxes(b_spatial, 1, 2), HEAD_DIM, axis=2),
        "sink": sink, "tab": tab, "tab_prev": tab_prev, "tab_next": tab_next, "wo": wo,
        "g2": norm_ffn2[:, None, :], "wg2": wg2, "wu2": wu2, "wd2": wd2,
        "gf": norm_final[None, :],
    }


def _forward(xs, w):
    geo = _Geometry([x.shape[:2] for x in xs])
    acts = [x.reshape(-1, D_MODEL) for x in xs]
    for layer in range(DEPTH):
        x1, q, kx, vx, c = _mix_in(acts, layer, w, geo)
        acts = _mix_out(x1, q, kx, vx, c, layer, w, geo, last_layer=(layer == DEPTH - 1))
    return tuple(y.reshape(x.shape) for x, y in zip(xs, acts))


def kernel(x_prompt, x_sample, norm_ffn1, w1_gate, w1_up, w1_down, norm_mix, w_in, sink, gmlp_v_gain, w_spatial, b_spatial, w_out, norm_ffn2, w2_gate, w2_up, w2_down, norm_final):
    w = _prepare(norm_ffn1, w1_gate, w1_up, w1_down, norm_mix, w_in, sink, gmlp_v_gain, w_spatial,
                 b_spatial, w_out, norm_ffn2, w2_gate, w2_up, w2_down, norm_final)
    return _forward((x_prompt, x_sample), w)
```
